```python
import jax, jax.numpy as jnp
from jax import lax
import numpy as np

D_MODEL = 2048
BATCH = 2
SEQ = 8192
DEPTH = 1

CHUNK = 64
Q_BLOCK = 128
MLA_HEADS = 8
MLA_NOPE = 128
MLA_ROPE = 64
MLA_V = 128
Q_LORA = 512
KV_LORA = 256
ROPE_BASE = 10000.0
MLA_OUT = MLA_HEADS * MLA_V
ML_HEADS = 4
ML_QK = 128
ML_V = 256
ML_OUT = ML_HEADS * ML_V
GATE_CAP = 15.0
D_MIX = MLA_OUT + ML_OUT
IN_SIZES = (Q_LORA, KV_LORA, MLA_ROPE, ML_HEADS * ML_QK, ML_HEADS * ML_QK, ML_OUT, ML_OUT, ML_HEADS, ML_HEADS)
IN_WIDTH = Q_LORA + KV_LORA + MLA_ROPE + 2 * ML_HEADS * ML_QK + 2 * ML_OUT + 2 * ML_HEADS
D_FF = 5632
CONV_W = 3
RMS_EPS = 1e-6
LN_EPS = 1e-5
DN_ALPHA = (2 * DEPTH) ** 0.25
DN_BETA = (8 * DEPTH) ** -0.25

kernel_name = "hybrid_mla_mlstm_convglu_deepnorm_block"


def _rmsnorm(x, g):
    xf = x.astype(jnp.float32)
    y = xf * lax.rsqrt(jnp.mean(xf * xf, axis=-1, keepdims=True) + RMS_EPS)
    return (y * g.astype(jnp.float32)).astype(x.dtype)


def _layernorm(x, g, b):
    xf = x.astype(jnp.float32)
    mu = jnp.mean(xf, axis=-1, keepdims=True)
    xc = xf - mu
    y = xc * lax.rsqrt(jnp.mean(xc * xc, axis=-1, keepdims=True) + LN_EPS)
    return (y * g.astype(jnp.float32) + b.astype(jnp.float32)).astype(x.dtype)


def _rope(t, cos, sin):
    half = t.shape[-1] // 2
    t1 = t[..., :half].astype(jnp.float32)
    t2 = t[..., half:].astype(jnp.float32)
    return jnp.concatenate([t1 * cos - t2 * sin, t1 * sin + t2 * cos], axis=-1).astype(t.dtype)


def _mla(c_q, c_kv, k_rope_raw, positions, q_norm_g, kv_norm_g, w_uq, w_uk, w_uv):
    B, S, _ = c_q.shape
    cq = _rmsnorm(c_q, q_norm_g)
    ckv = _rmsnorm(c_kv, kv_norm_g)
    q = (cq @ w_uq).reshape(B, S, MLA_HEADS, MLA_NOPE + MLA_ROPE)
    q_nope, q_rope = q[..., :MLA_NOPE], q[..., MLA_NOPE:]
    k_nope = (ckv @ w_uk).reshape(B, S, MLA_HEADS, MLA_NOPE)
    v = (ckv @ w_uv).reshape(B, S, MLA_HEADS, MLA_V)
    inv_freq = 1.0 / (ROPE_BASE ** (jnp.arange(0, MLA_ROPE, 2, dtype=jnp.float32) / MLA_ROPE))
    ang = positions.astype(jnp.float32)[..., None] * inv_freq
    cos, sin = jnp.cos(ang), jnp.sin(ang)
    q_rope = _rope(q_rope, cos[:, :, None, :], sin[:, :, None, :])
    k_rope = _rope(k_rope_raw, cos, sin)
    scale = (MLA_NOPE + MLA_ROPE) ** -0.5
    key_chunk = jnp.arange(S) // CHUNK

    def attend_block(i):
        start = i * Q_BLOCK
        qn = lax.dynamic_slice_in_dim(q_nope, start, Q_BLOCK, axis=1)
        qr = lax.dynamic_slice_in_dim(q_rope, start, Q_BLOCK, axis=1)
        s = (jnp.einsum('bqhd,bkhd->bhqk', qn, k_nope, preferred_element_type=jnp.float32)
             + jnp.einsum('bqhr,bkr->bhqk', qr, k_rope, preferred_element_type=jnp.float32))
        q_chunk = (start + jnp.arange(Q_BLOCK)) // CHUNK
        allowed = key_chunk[None, :] <= q_chunk[:, None]
        s = jnp.where(allowed, s * scale, -jnp.inf)
        p = jax.nn.softmax(s, axis=-1).astype(v.dtype)
        return jnp.einsum('bhqk,bkhd->bqhd', p, v)

    out = lax.map(attend_block, jnp.arange(S // Q_BLOCK))
    return out.transpose(1, 0, 2, 3, 4).reshape(B, S, MLA_OUT)


def _mlstm(q, k, v, i_pre, f_pre, o_pre, head_g):
    B, S, _ = q.shape
    NC = S // CHUNK
    dt = q.dtype

    def to_chunks(t, d):
        return t.astype(jnp.float32).reshape(B, NC, CHUNK, ML_HEADS, d).transpose(1, 0, 3, 2, 4)

    def gate_chunks(t):
        return t.reshape(B, NC, CHUNK, ML_HEADS).transpose(1, 0, 3, 2)

    qc = to_chunks(q, ML_QK)
    kc = to_chunks(k, ML_QK) * (ML_QK ** -0.5)
    vc = to_chunks(v, ML_V)
    ig = GATE_CAP * jnp.tanh(i_pre.astype(jnp.float32) / GATE_CAP)
    lf = jax.nn.log_sigmoid(GATE_CAP * jnp.tanh(f_pre.astype(jnp.float32) / GATE_CAP))
    igc, lfc = gate_chunks(ig), gate_chunks(lf)
    causal = jnp.tril(jnp.ones((CHUNK, CHUNK), dtype=bool))

    def step(carry, xs):
        C, n, m_prev = carry
        qb, kb, vb, igb, lfb = xs
        b = jnp.cumsum(lfb, axis=-1)
        dlog = jnp.where(causal, b[..., :, None] - b[..., None, :] + igb[..., None, :], -jnp.inf)
        inter = b + m_prev[..., None]
        m = jnp.maximum(inter, jnp.max(dlog, axis=-1))
        s = jnp.einsum('bhtd,bhsd->bhts', qb, kb) * jnp.exp(dlog - m[..., None])
        inter_w = jnp.exp(inter - m)
        num = (jnp.einsum('bhts,bhsv->bhtv', s, vb)
               + inter_w[..., None] * jnp.einsum('bhtd,bhvd->bhtv', qb, C))
        den = jnp.sum(s, axis=-1) + inter_w * jnp.einsum('bhtd,bhd->bht', qb, n)
        h = num / jnp.maximum(jnp.abs(den), jnp.exp(-m))[..., None]
        m_new = m[..., -1]
        decay = jnp.exp(inter[..., -1] - m_new)
        w = jnp.exp(b[..., -1:] - b + igb - m_new[..., None])
        C_new = decay[..., None, None] * C + jnp.einsum('bhs,bhsv,bhsd->bhvd', w, vb, kb)
        n_new = decay[..., None] * n + jnp.einsum('bhs,bhsd->bhd', w, kb)
        return (C_new, n_new, m_new), h

    init = (jnp.zeros((B, ML_HEADS, ML_V, ML_QK), jnp.float32),
            jnp.zeros((B, ML_HEADS, ML_QK), jnp.float32),
            jnp.full((B, ML_HEADS), -jnp.inf, jnp.float32))
    _, h = lax.scan(step, init, (qc, kc, vc, igc, lfc))
    h = h.transpose(1, 0, 3, 2, 4).reshape(B, S, ML_HEADS, ML_V)
    h = _rmsnorm(h, head_g.reshape(ML_HEADS, ML_V)).reshape(B, S, ML_OUT)
    return (jax.nn.sigmoid(o_pre.astype(jnp.float32)) * h).astype(dt)


def _causal_dwconv(u, w, b):
    out = lax.conv_general_dilated(
        u, w[:, None, :].astype(u.dtype), window_strides=(1,), padding=[(CONV_W - 1, 0)],
        dimension_numbers=('NWC', 'WIO', 'NWC'), feature_group_count=u.shape[-1])
    return out + b


def setup_inputs(seed: int = 0) -> dict:
    key = jax.random.key(seed)
    ks = jax.random.split(key, 24)
    f32 = jnp.float32

    def nrm(k, shape, scale):
        return jax.random.normal(k, shape, f32) * scale

    def gain(k, n):
        return 1.0 + nrm(k, (DEPTH, n), 0.02)

    x = jax.random.normal(ks[0], (BATCH, SEQ, D_MODEL), f32)
    offset = jax.random.randint(ks[1], (BATCH, 1), 0, 64, dtype=jnp.int32) * CHUNK
    positions = (offset + jnp.arange(SEQ, dtype=jnp.int32)[None, :]).astype(jnp.int32)
    f_bias = jnp.linspace(3.0, 6.0, ML_HEADS, dtype=f32)[None, :] + nrm(ks[9], (DEPTH, ML_HEADS), 0.1)
    return {
        "x": x,
        "positions": positions,
        "w_in": nrm(ks[2], (DEPTH, D_MODEL, IN_WIDTH), D_MODEL ** -0.5),
        "q_norm_g": gain(ks[3], Q_LORA),
        "kv_norm_g": gain(ks[4], KV_LORA),
        "w_uq": nrm(ks[5], (DEPTH, Q_LORA, MLA_HEADS * (MLA_NOPE + MLA_ROPE)), Q_LORA ** -0.5),
        "w_uk": nrm(ks[6], (DEPTH, KV_LORA, MLA_HEADS * MLA_NOPE), KV_LORA ** -0.5),
        "w_uv": nrm(ks[7], (DEPTH, KV_LORA, MLA_OUT), KV_LORA ** -0.5 * DN_BETA),
        "b_igate": nrm(ks[8], (DEPTH, ML_HEADS), 0.1),
        "b_fgate": f_bias,
        "ml_head_g": gain(ks[10], ML_OUT),
        "beta_mla": gain(ks[11], MLA_OUT),
        "beta_ml": gain(ks[12], ML_OUT),
        "w_out": nrm(ks[13], (DEPTH, D_MIX, D_MODEL), D_MIX ** -0.5 * DN_BETA),
        "ln1_g": gain(ks[14], D_MODEL),
        "ln1_b": nrm(ks[15], (DEPTH, D_MODEL), 0.02),
        "w_ffn_gate": nrm(ks[16], (DEPTH, D_MODEL, D_FF), D_MODEL ** -0.5),
        "w_ffn_val": nrm(ks[17], (DEPTH, D_MODEL, D_FF), D_MODEL ** -0.5),
        "conv_w": nrm(ks[18], (DEPTH, CONV_W, D_FF), CONV_W ** -0.5),
        "conv_b": nrm(ks[19], (DEPTH, D_FF), 0.02),
        "w_down": nrm(ks[20], (DEPTH, D_FF, D_MODEL), D_FF ** -0.5 * DN_BETA),
        "ln2_g": gain(ks[21], D_MODEL),
        "ln2_b": nrm(ks[22], (DEPTH, D_MODEL), 0.02),
    }


def reference(x, positions, w_in, q_norm_g, kv_norm_g, w_uq, w_uk, w_uv, b_igate, b_fgate,
              ml_head_g, beta_mla, beta_ml, w_out, ln1_g, ln1_b, w_ffn_gate, w_ffn_val,
              conv_w, conv_b, w_down, ln2_g, ln2_b):
    split_points = [int(p) for p in np.cumsum(IN_SIZES)[:-1]]
    h = x
    for l in range(DEPTH):
        proj = h @ w_in[l]
        c_q, c_kv, k_rope, m_q, m_k, m_v, m_o, m_i, m_f = jnp.split(proj, split_points, axis=-1)
        attn = _mla(c_q, c_kv, k_rope, positions, q_norm_g[l], kv_norm_g[l], w_uq[l], w_uk[l], w_uv[l])
        mem = _mlstm(m_q, m_k, m_v, m_i + b_igate[l], m_f + b_fgate[l], m_o, ml_head_g[l])
        mix = jnp.concatenate([attn * beta_mla[l], mem * beta_ml[l]], axis=-1)
        h = _layernorm(DN_ALPHA * h + mix @ w_out[l], ln1_g[l], ln1_b[l])
        gate = _causal_dwconv(h @ w_ffn_gate[l], conv_w[l], conv_b[l])
        ffn = (jax.nn.gelu(gate) * (h @ w_ffn_val[l])) @ w_down[l]
        h = _layernorm(DN_ALPHA * h + ffn, ln2_g[l], ln2_b[l])
    return h
```

```python
import functools
import math

import jax
import jax.numpy as jnp
from jax import lax
from jax.experimental import pallas as pl
from jax.experimental.pallas import tpu as pltpu

D_MODEL = 2048
CHUNK = 64
MLA_HEADS = 8
MLA_NOPE = 128
MLA_ROPE = 64
MLA_V = 128
Q_LORA = 512
KV_LORA = 256
ROPE_BASE = 10000.0
MLA_OUT = MLA_HEADS * MLA_V
ML_HEADS = 4
ML_QK = 128
ML_V = 256
ML_OUT = ML_HEADS * ML_V
GATE_CAP = 15.0
D_FF = 5632
RMS_EPS = 1e-6
LN_EPS = 1e-5
DEPTH = 1
DN_ALPHA = (2 * DEPTH) ** 0.25

LANES = 128
V7X_VMEM_LIMIT_BYTES = 58 * 1024 * 1024

PROJ_TM = 256
ATT_T = 512
ML_L = 256
OUT_TM = 512
FFN_TM = 512
FFN_TF = 512

HEAD_PAD = 2 * LANES
PROJ_W = 4096
_C_CQ, _C_CKV, _C_KR, _C_KROT, _C_MQ, _C_MK, _C_MV, _C_MO = 0, 512, 768, 896, 1024, 1536, 2048, 3072

F32 = jnp.float32
BF16 = jnp.bfloat16
NEG_BIG = -1e30


def _nt_dot(a, b):
    return lax.dot_general(a, b, (((1,), (1,)), ((), ())), preferred_element_type=F32)


def _dot(a, b):
    return jnp.dot(a, b, preferred_element_type=F32)


def _const_spec(shape):
    nd = len(shape)
    return pl.BlockSpec(shape, lambda *_: (0,) * nd, pipeline_mode=pl.Buffered(1))


def _proj_kernel(x_ref, pos_ref, invf_ref, w_ref, wg_ref, gb_ref, qg_ref, kvg_ref,
                 wuq_ref, wuk_ref, wuv_ref,
                 q_ref, k_ref, v_ref, mq_ref, mk_ref, mv_ref, mo_ref, gt_ref,
                 xb_ref):
    xb_ref[...] = x_ref[...].astype(BF16)
    xb = xb_ref[...]
    tm = xb.shape[0]

    ang = pos_ref[...] * invf_ref[...]
    cos = jnp.cos(ang)
    sin = jnp.sin(ang)

    cq = _dot(xb, w_ref[:, _C_CQ:_C_CQ + Q_LORA])
    cq = cq * lax.rsqrt(jnp.mean(cq * cq, axis=-1, keepdims=True) + RMS_EPS) * qg_ref[...]
    cqb = cq.astype(BF16)
    qscale = (MLA_NOPE + MLA_ROPE) ** -0.5 * math.log2(math.e)
    q_nope = _dot(cqb, wuq_ref[:, 0:1024]) * qscale
    cos8 = jnp.concatenate([cos] * MLA_HEADS, axis=1)
    sin8 = jnp.concatenate([sin] * MLA_HEADS, axis=1)
    q_rope = (_dot(cqb, wuq_ref[:, 1024:2048]) * cos8 + _dot(cqb, wuq_ref[:, 2048:3072]) * sin8) * qscale
    for h in range(MLA_HEADS):
        q_ref[:, h * HEAD_PAD:h * HEAD_PAD + LANES] = q_nope[:, h * LANES:(h + 1) * LANES].astype(BF16)
        q_ref[:, h * HEAD_PAD + LANES:(h + 1) * HEAD_PAD] = q_rope[:, h * LANES:(h + 1) * LANES].astype(BF16)

    p1 = _dot(xb, w_ref[:, _C_CKV:_C_MQ])
    ckv = p1[:, 0:KV_LORA]
    ckv = ckv * lax.rsqrt(jnp.mean(ckv * ckv, axis=-1, keepdims=True) + RMS_EPS) * kvg_ref[...]
    ckvb = ckv.astype(BF16)
    k_rope = (p1[:, 256:384] * cos + p1[:, 384:512] * sin).astype(BF16)
    k_nope = _dot(ckvb, wuk_ref[...]).astype(BF16)
    for h in range(MLA_HEADS):
        k_ref[:, h * HEAD_PAD:h * HEAD_PAD + LANES] = k_nope[:, h * LANES:(h + 1) * LANES]
        k_ref[:, h * HEAD_PAD + LANES:(h + 1) * HEAD_PAD] = k_rope
    v_ref[...] = _dot(ckvb, wuv_ref[...]).astype(BF16)

    mq_ref[...] = _dot(xb, w_ref[:, _C_MQ:_C_MK]).astype(BF16)
    mk_ref[...] = (_dot(xb, w_ref[:, _C_MK:_C_MV]) * (ML_QK ** -0.5)).astype(BF16)
    mv_ref[...] = _dot(xb, w_ref[:, _C_MV:_C_MO]).astype(BF16)
    mo_ref[...] = _dot(xb, w_ref[:, _C_MO:PROJ_W]).astype(BF16)
    gt_ref[...] = _nt_dot(wg_ref[...], xb) + gb_ref[...]


def _proj_call(x2, pos_col, invf, w_all, wg_t, gbias, qg, kvg, wuq_all, wuk, wuv):
    T = x2.shape[0]
    tm = PROJ_TM
    row = lambda w: pl.BlockSpec((tm, w), lambda i: (i, 0))
    out_shapes = (
        jax.ShapeDtypeStruct((T, MLA_HEADS * HEAD_PAD), BF16),
        jax.ShapeDtypeStruct((T, MLA_HEADS * HEAD_PAD), BF16),
        jax.ShapeDtypeStruct((T, MLA_OUT), BF16),
        jax.ShapeDtypeStruct((T, ML_HEADS * ML_QK), BF16),
        jax.ShapeDtypeStruct((T, ML_HEADS * ML_QK), BF16),
        jax.ShapeDtypeStruct((T, ML_OUT), BF16),
        jax.ShapeDtypeStruct((T, ML_OUT), BF16),
        jax.ShapeDtypeStruct((8, T), F32),
    )
    return pl.pallas_call(
        _proj_kernel,
        grid=(T // tm,),
        in_specs=[
            row(D_MODEL),
            pl.BlockSpec((tm, 1), lambda i: (i, 0)),
            _const_spec((1, LANES)),
            _const_spec((D_MODEL, PROJ_W)),
            _const_spec((8, D_MODEL)),
            _const_spec((8, 1)),
            _const_spec((1, Q_LORA)),
            _const_spec((1, KV_LORA)),
            _const_spec((Q_LORA, 3072)),
            _const_spec((KV_LORA, MLA_HEADS * MLA_NOPE)),
            _const_spec((KV_LORA, MLA_OUT)),
        ],
        out_specs=(
            row(MLA_HEADS * HEAD_PAD), row(MLA_HEADS * HEAD_PAD), row(MLA_OUT),
            row(ML_HEADS * ML_QK), row(ML_HEADS * ML_QK), row(ML_OUT), row(ML_OUT),
            pl.BlockSpec((8, tm), lambda i: (0, i)),
        ),
        out_shape=out_shapes,
        scratch_shapes=[pltpu.VMEM((tm, D_MODEL), BF16)],
        compiler_params=pltpu.CompilerParams(
            dimension_semantics=("arbitrary",), vmem_limit_bytes=V7X_VMEM_LIMIT_BYTES),
        name="proj_mla_prep",
    )(x2, pos_col, invf, w_all, wg_t, gbias, qg, kvg, wuq_all, wuk, wuv)


def _flash_kernel(q_ref, k_ref, v_ref, beta_ref, o_ref, m_sc, l_sc, acc_sc):
    i = pl.program_id(2)
    t = ATT_T
    q = q_ref[...]
    m_sc[...] = jnp.full(m_sc.shape, NEG_BIG, F32)
    l_sc[...] = jnp.zeros(l_sc.shape, F32)
    acc_sc[...] = jnp.zeros(acc_sc.shape, F32)

    def block(j, masked):
        off = pl.multiple_of(j * t, t)
        k = k_ref[pl.ds(off, t), :]
        v = v_ref[pl.ds(off, t), :]
        s = _nt_dot(q, k)
        if masked:
            r = lax.broadcasted_iota(jnp.int32, (t, t), 0) // CHUNK
            c = lax.broadcasted_iota(jnp.int32, (t, t), 1) // CHUNK
            s = jnp.where(c <= r, s, -jnp.inf)
        m_prev = m_sc[...]
        m_new = jnp.maximum(m_prev, jnp.max(s, axis=1, keepdims=True))
        alpha = jnp.exp2(m_prev - m_new)
        p = jnp.exp2(s - pltpu.repeat(m_new, t // LANES, axis=1))
        l_sc[...] = alpha * l_sc[...] + jnp.sum(p, axis=1, keepdims=True)
        acc_sc[...] = alpha * acc_sc[...] + _dot(p.astype(BF16), v)
        m_sc[...] = m_new

    def body(j, carry):
        block(j, False)
        return carry

    lax.fori_loop(0, i, body, 0)
    block(i, True)
    o_ref[...] = (acc_sc[...] / l_sc[...] * beta_ref[...]).astype(BF16)


def _flash_call(q_pad, k_pad, v, beta_mla, batch, seq):
    T = q_pad.shape[0]
    t = ATT_T
    nq = seq // t
    return pl.pallas_call(
        _flash_kernel,
        grid=(batch, MLA_HEADS, nq),
        in_specs=[
            pl.BlockSpec((t, HEAD_PAD), lambda b, h, i: (b * nq + i, h)),
            pl.BlockSpec((seq, HEAD_PAD), lambda b, h, i: (b, h)),
            pl.BlockSpec((seq, MLA_V), lambda b, h, i: (b, h)),
            pl.BlockSpec((1, MLA_V), lambda b, h, i: (0, h)),
        ],
        out_specs=pl.BlockSpec((t, MLA_V), lambda b, h, i: (b * nq + i, h)),
        out_shape=jax.ShapeDtypeStruct((T, MLA_OUT), BF16),
        scratch_shapes=[pltpu.VMEM((t, LANES), F32), pltpu.VMEM((t, LANES), F32),
                        pltpu.VMEM((t, MLA_V), F32)],
        compiler_params=pltpu.CompilerParams(
            dimension_semantics=("arbitrary", "arbitrary", "arbitrary"),
            vmem_limit_bytes=V7X_VMEM_LIMIT_BYTES),
        name="mla_flash",
    )(q_pad, k_pad, v, beta_mla)


def _lane_cumsum(x):
    n = x.shape[-1]
    lane = lax.broadcasted_iota(jnp.int32, x.shape, x.ndim - 1)
    d = 1
    while d < n:
        x = x + jnp.where(lane >= d, pltpu.roll(x, d, x.ndim - 1), 0.0)
        d *= 2
    return x


def _mlstm_kernel(q_ref, k_ref, v_ref, o_ref, gt_ref, hg_ref, beta_ref, out_ref,
                  c_sc, n_sc, m_sc):
    L = ML_L

    @pl.when(pl.program_id(1) == 0)
    def _():
        c_sc[...] = jnp.zeros(c_sc.shape, F32)
        n_sc[...] = jnp.zeros(n_sc.shape, F32)
        m_sc[...] = jnp.full(m_sc.shape, -jnp.inf, F32)

    g8 = gt_ref[...]
    capped = GATE_CAP * jnp.tanh(g8 / GATE_CAP)
    logf = jnp.minimum(capped, 0.0) - jnp.log1p(jnp.exp(-jnp.abs(capped)))
    row8 = lax.broadcasted_iota(jnp.int32, g8.shape, 0)
    bsum = _lane_cumsum(jnp.where(row8 >= ML_HEADS, logf, 0.0))
    gb8 = jnp.where(row8 < ML_HEADS, capped - pltpu.roll(bsum, ML_HEADS, 0), bsum)

    rr = lax.broadcasted_iota(jnp.int32, (L, L), 0)
    cc = lax.broadcasted_iota(jnp.int32, (L, L), 1)
    causal = cc <= rr
    eye = cc == rr

    for h in range(ML_HEADS):
        g_row = gb8[h:h + 1, :]
        b_row = gb8[ML_HEADS + h:ML_HEADS + h + 1, :]
        G = jnp.broadcast_to(g_row, (L, L))
        Bm = jnp.broadcast_to(b_row, (L, L))
        m_prev = m_sc[h:h + 1, 0:1]
        cummax = jnp.max(jnp.where(causal, G, -jnp.inf), axis=1, keepdims=True)
        mcol = jnp.maximum(m_prev, cummax)
        g_col = jnp.sum(jnp.where(eye, G, 0.0), axis=1, keepdims=True)
        b_col = jnp.sum(jnp.where(eye, Bm, 0.0), axis=1, keepdims=True)
        dmat = jnp.where(causal, jnp.exp(G - mcol), 0.0)
        inter_w = jnp.exp(m_prev - mcol)

        qh = q_ref[:, h * ML_QK:(h + 1) * ML_QK]
        kh = k_ref[:, h * ML_QK:(h + 1) * ML_QK]
        vh = v_ref[:, h * ML_V:(h + 1) * ML_V]
        ct = c_sc[h]
        nrow = n_sc[h:h + 1, :]

        s = _nt_dot(qh, kh) * dmat
        num = _dot(s.astype(BF16), vh) + inter_w * _dot(qh, ct.astype(BF16))
        den = (jnp.sum(s, axis=1, keepdims=True)
               + inter_w * jnp.sum(qh.astype(F32) * nrow, axis=1, keepdims=True))
        hval = num / jnp.maximum(jnp.abs(den), jnp.exp(-(b_col + mcol)))

        m_last = mcol[L - 1:L, :]
        decay = inter_w[L - 1:L, :]
        w_col = jnp.exp(g_col - m_last)
        kw = kh.astype(F32) * w_col
        upd = lax.dot_general(kw.astype(BF16), vh, (((0,), (0,)), ((), ())),
                              preferred_element_type=F32)
        c_sc[h] = decay * ct + upd
        n_sc[h:h + 1, :] = decay * nrow + jnp.sum(kw, axis=0, keepdims=True)
        m_sc[h:h + 1, :] = jnp.broadcast_to(b_row[:, L - 1:L] + m_last, (1, LANES))

        hn = hval * lax.rsqrt(jnp.mean(hval * hval, axis=-1, keepdims=True) + RMS_EPS)
        hn = hn * hg_ref[:, h * ML_V:(h + 1) * ML_V]
        og = jax.nn.sigmoid(o_ref[:, h * ML_V:(h + 1) * ML_V].astype(F32))
        out_ref[:, h * ML_V:(h + 1) * ML_V] = (
            og * hn * beta_ref[:, h * ML_V:(h + 1) * ML_V]).astype(BF16)


def _mlstm_call(mq, mk, mv, mo, gates_t, head_g, beta_ml, batch, seq):
    T = mq.shape[0]
    L = ML_L
    nc = seq // L
    blk = lambda w: pl.BlockSpec((L, w), lambda b, c: (b * nc + c, 0))
    return pl.pallas_call(
        _mlstm_kernel,
        grid=(batch, nc),
        in_specs=[
            blk(ML_HEADS * ML_QK), blk(ML_HEADS * ML_QK), blk(ML_OUT), blk(ML_OUT),
            pl.BlockSpec((8, L), lambda b, c: (0, b * nc + c)),
            _const_spec((1, ML_OUT)),
            _const_spec((1, ML_OUT)),
        ],
        out_specs=blk(ML_OUT),
        out_shape=jax.ShapeDtypeStruct((T, ML_OUT), BF16),
        scratch_shapes=[pltpu.VMEM((ML_HEADS, ML_QK, ML_V), F32),
                        pltpu.VMEM((8, ML_QK), F32),
                        pltpu.VMEM((8, LANES), F32)],
        compiler_params=pltpu.CompilerParams(
            dimension_semantics=("arbitrary", "arbitrary"),
            vmem_limit_bytes=V7X_VMEM_LIMIT_BYTES),
        name="mlstm_scan",
    )(mq, mk, mv, mo, gates_t, head_g, beta_ml)


def _layernorm(y, g, b):
    mu = jnp.mean(y, axis=-1, keepdims=True)
    yc = y - mu
    return yc * lax.rsqrt(jnp.mean(yc * yc, axis=-1, keepdims=True) + LN_EPS) * g + b


def _outproj_kernel(a_ref, m_ref, x_ref, w_ref, g_ref, b_ref, o_ref):
    acc = _dot(a_ref[...], w_ref[0:MLA_OUT, :]) + _dot(m_ref[...], w_ref[MLA_OUT:, :])
    o_ref[...] = _layernorm(DN_ALPHA * x_ref[...] + acc, g_ref[...], b_ref[...])


def _outproj_call(attn, mem, x2, w_out, g, b):
    T = x2.shape[0]
    tm = OUT_TM
    return pl.pallas_call(
        _outproj_kernel,
        grid=(T // tm,),
        in_specs=[
            pl.BlockSpec((tm, MLA_OUT), lambda i: (i, 0)),
            pl.BlockSpec((tm, ML_OUT), lambda i: (i, 0)),
            pl.BlockSpec((tm, D_MODEL), lambda i: (i, 0)),
            _const_spec((MLA_OUT + ML_OUT, D_MODEL)),
            _const_spec((1, D_MODEL)),
            _const_spec((1, D_MODEL)),
        ],
        out_specs=pl.BlockSpec((tm, D_MODEL), lambda i: (i, 0)),
        out_shape=jax.ShapeDtypeStruct((T, D_MODEL), F32),
        compiler_params=pltpu.CompilerParams(
            dimension_semantics=("arbitrary",), vmem_limit_bytes=V7X_VMEM_LIMIT_BYTES),
        name="outproj_ln",
    )(attn, mem, x2, w_out, g, b)


def _gelu_tanh(x):
    c = math.sqrt(2.0 / math.pi)
    return 0.5 * x * (1.0 + jnp.tanh(c * (x + 0.044715 * (x * x * x))))


def _ffn_kernel(tiles_per_seq, h_ref, wg_ref, wv_ref, wd_ref, cw_ref, cb_ref, g_ref, b_ref,
                o_ref, hb_ref, gbuf_ref, halo_ref):
    i = pl.program_id(0)
    f = pl.program_id(1)
    nf = pl.num_programs(1)
    tm = FFN_TM

    @pl.when(f == 0)
    def _():
        hb_ref[...] = h_ref[...].astype(BF16)
        o_ref[...] = DN_ALPHA * h_ref[...]

    hb = hb_ref[...]
    gate = _dot(hb, wg_ref[...])
    val = _dot(hb, wv_ref[...])

    first = (i % tiles_per_seq) == 0

    @pl.when(first)
    def _():
        gbuf_ref[0:8, :] = jnp.zeros((8, gate.shape[1]), F32)

    @pl.when(jnp.logical_not(first))
    def _():
        gbuf_ref[0:8, :] = halo_ref[f]

    gbuf_ref[8:8 + tm, :] = gate
    halo_ref[f] = gate[tm - 8:tm, :]
    conv = (cw_ref[0:1, :] * gbuf_ref[6:6 + tm, :] + cw_ref[1:2, :] * gbuf_ref[7:7 + tm, :]
            + cw_ref[2:3, :] * gate + cb_ref[...])
    act = (_gelu_tanh(conv) * val).astype(BF16)
    o_ref[...] += _dot(act, wd_ref[...])

    @pl.when(f == nf - 1)
    def _():
        o_ref[...] = _layernorm(o_ref[...], g_ref[...], b_ref[...])


def _ffn_call(h1, wg, wv, wd, conv_w, conv_b, g, b, seq):
    T = h1.shape[0]
    tm, tf = FFN_TM, FFN_TF
    nf = D_FF // tf
    kern = functools.partial(_ffn_kernel, seq // tm)
    return pl.pallas_call(
        kern,
        grid=(T // tm, nf),
        in_specs=[
            pl.BlockSpec((tm, D_MODEL), lambda i, f: (i, 0)),
            pl.BlockSpec((D_MODEL, tf), lambda i, f: (0, f)),
            pl.BlockSpec((D_MODEL, tf), lambda i, f: (0, f)),
            pl.BlockSpec((tf, D_MODEL), lambda i, f: (f, 0)),
            pl.BlockSpec((3, tf), lambda i, f: (0, f)),
            pl.BlockSpec((1, tf), lambda i, f: (0, f)),
            _const_spec((1, D_MODEL)),
            _const_spec((1, D_MODEL)),
        ],
        out_specs=pl.BlockSpec((tm, D_MODEL), lambda i, f: (i, 0)),
        out_shape=jax.ShapeDtypeStruct((T, D_MODEL), F32),
        scratch_shapes=[pltpu.VMEM((tm, D_MODEL), BF16),
                        pltpu.VMEM((tm + 8, tf), F32),
                        pltpu.VMEM((nf, 8, tf), F32)],
        compiler_params=pltpu.CompilerParams(
            dimension_semantics=("arbitrary", "arbitrary"),
            vmem_limit_bytes=V7X_VMEM_LIMIT_BYTES),
        name="convglu_ffn_ln",
    )(h1, wg, wv, wd, conv_w, conv_b, g, b)


def _prep_proj_weights(w_in, w_uq, b_igate, b_fgate):
    half = MLA_ROPE // 2
    o = 0
    w_cq = w_in[:, o:o + Q_LORA]; o += Q_LORA
    w_ckv = w_in[:, o:o + KV_LORA]; o += KV_LORA
    w_kr = w_in[:, o:o + MLA_ROPE]; o += MLA_ROPE
    w_mq = w_in[:, o:o + ML_HEADS * ML_QK]; o += ML_HEADS * ML_QK
    w_mk = w_in[:, o:o + ML_HEADS * ML_QK]; o += ML_HEADS * ML_QK
    w_mv = w_in[:, o:o + ML_OUT]; o += ML_OUT
    w_mo = w_in[:, o:o + ML_OUT]; o += ML_OUT
    w_gates = w_in[:, o:o + 2 * ML_HEADS]
    w_krot = jnp.concatenate([-w_kr[:, half:], w_kr[:, :half]], axis=1)
    z64 = jnp.zeros((D_MODEL, LANES - MLA_ROPE), w_in.dtype)
    w_all = jnp.concatenate([w_cq, w_ckv, w_kr, z64, w_krot, z64, w_mq, w_mk, w_mv, w_mo], axis=1)
    wg_t = w_gates.T
    gbias = jnp.concatenate([b_igate, b_fgate])[:, None]

    uq = w_uq.reshape(Q_LORA, MLA_HEADS, MLA_NOPE + MLA_ROPE)
    uq_nope = uq[:, :, :MLA_NOPE].reshape(Q_LORA, MLA_HEADS * MLA_NOPE)
    uq_r = uq[:, :, MLA_NOPE:]
    uq_rot = jnp.concatenate([-uq_r[:, :, half:], uq_r[:, :, :half]], axis=2)
    zpad = jnp.zeros((Q_LORA, MLA_HEADS, LANES - MLA_ROPE), w_uq.dtype)
    uq_r = jnp.concatenate([uq_r, zpad], axis=2).reshape(Q_LORA, MLA_HEADS * LANES)
    uq_rot = jnp.concatenate([uq_rot, zpad], axis=2).reshape(Q_LORA, MLA_HEADS * LANES)
    wuq_all = jnp.concatenate([uq_nope, uq_r, uq_rot], axis=1)
    return w_all.astype(BF16), wg_t.astype(BF16), gbias.astype(F32), wuq_all.astype(BF16)


def kernel(x, positions, w_in, q_norm_g, kv_norm_g, w_uq, w_uk, w_uv, b_igate, b_fgate,
           ml_head_g, beta_mla, beta_ml, w_out, ln1_g, ln1_b, w_ffn_gate, w_ffn_val,
           conv_w, conv_b, w_down, ln2_g, ln2_b):
    B, S, D = x.shape
    T = B * S
    assert D == D_MODEL and w_in.shape[0] == DEPTH == 1
    assert S % ATT_T == 0 and S % ML_L == 0 and S % FFN_TM == 0 and T % PROJ_TM == 0
    l = 0
    x2 = x.reshape(T, D)
    pos_col = positions.reshape(T, 1).astype(F32)
    inv_freq = 1.0 / (ROPE_BASE ** (jnp.arange(0, MLA_ROPE, 2, dtype=F32) / MLA_ROPE))
    invf = jnp.tile(inv_freq, LANES // (MLA_ROPE // 2))[None, :]

    w_all, wg_t, gbias, wuq_all = _prep_proj_weights(w_in[l], w_uq[l], b_igate[l], b_fgate[l])
    q_pad, k_pad, v, mq, mk, mv, mo, gates_t = _proj_call(
        x2, pos_col, invf, w_all, wg_t, gbias,
        q_norm_g[l][None, :], kv_norm_g[l][None, :], wuq_all,
        w_uk[l].astype(BF16), w_uv[l].astype(BF16))

    attn = _flash_call(q_pad, k_pad, v, beta_mla[l][None, :], B, S)
    mem = _mlstm_call(mq, mk, mv, mo, gates_t, ml_head_g[l][None, :], beta_ml[l][None, :], B, S)
    h1 = _outproj_call(attn, mem, x2, w_out[l].astype(BF16), ln1_g[l][None, :], ln1_b[l][None, :])
    out = _ffn_call(h1, w_ffn_gate[l].astype(BF16), w_ffn_val[l].astype(BF16),
                    w_down[l].astype(BF16), conv_w[l], conv_b[l][None, :],
                    ln2_g[l][None, :], ln2_b[l][None, :], S)
    return out.reshape(B, S, D)
```

```python
import functools
import math

import jax
import jax.numpy as jnp
from jax import lax
from jax.experimental import pallas as pl
from jax.experimental.pallas import tpu as pltpu

D_MODEL = 2048
CHUNK = 64
MLA_HEADS = 8
MLA_NOPE = 128
MLA_ROPE = 64
MLA_V = 128
Q_LORA = 512
KV_LORA = 256
ROPE_BASE = 10000.0
MLA_OUT = MLA_HEADS * MLA_V
ML_HEADS = 4
ML_QK = 128
ML_V = 256
ML_OUT = ML_HEADS * ML_V
GATE_CAP = 15.0
D_FF = 5632
RMS_EPS = 1e-6
LN_EPS = 1e-5
DEPTH = 1
DN_ALPHA = (2 * DEPTH) ** 0.25

LANES = 128
V7X_VMEM_LIMIT_BYTES = 58 * 1024 * 1024

PROJ_TM = 256
ATT_T = 512
ML_L = 256
OUT_TM = 512
FFN_TM = 512
FFN_TF = 512

HEAD_PAD = 2 * LANES
PROJ_W = 4096
_C_CQ, _C_CKV, _C_KR, _C_KROT, _C_MQ, _C_MK, _C_MV, _C_MO = 0, 512, 768, 896, 1024, 1536, 2048, 3072

F32 = jnp.float32
BF16 = jnp.bfloat16
NEG_BIG = -1e30


def _nt_dot(a, b):
    return lax.dot_general(a, b, (((1,), (1,)), ((), ())), preferred_element_type=F32)


def _dot(a, b):
    return jnp.dot(a, b, preferred_element_type=F32)


def _const_spec(shape):
    nd = len(shape)
    return pl.BlockSpec(shape, lambda *_: (0,) * nd, pipeline_mode=pl.Buffered(1))


def _proj_kernel(x_ref, pos_ref, invf_ref, w_ref, wg_ref, gb_ref, qg_ref, kvg_ref,
                 wuq_ref, wuk_ref, wuv_ref,
                 q_ref, k_ref, v_ref, mq_ref, mk_ref, mv_ref, mo_ref, gt_ref,
                 xb_ref):
    xb_ref[...] = x_ref[...].astype(BF16)
    xb = xb_ref[...]
    tm = xb.shape[0]

    ang = pos_ref[...] * invf_ref[...]
    cos = jnp.cos(ang)
    sin = jnp.sin(ang)

    cq = _dot(xb, w_ref[:, _C_CQ:_C_CQ + Q_LORA])
    cq = cq * lax.rsqrt(jnp.mean(cq * cq, axis=-1, keepdims=True) + RMS_EPS) * qg_ref[...]
    cqb = cq.astype(BF16)
    qscale = (MLA_NOPE + MLA_ROPE) ** -0.5 * math.log2(math.e)
    q_nope = _dot(cqb, wuq_ref[:, 0:1024]) * qscale
    cos8 = jnp.concatenate([cos] * MLA_HEADS, axis=1)
    sin8 = jnp.concatenate([sin] * MLA_HEADS, axis=1)
    q_rope = (_dot(cqb, wuq_ref[:, 1024:2048]) * cos8 + _dot(cqb, wuq_ref[:, 2048:3072]) * sin8) * qscale
    for h in range(MLA_HEADS):
        q_ref[:, h * HEAD_PAD:h * HEAD_PAD + LANES] = q_nope[:, h * LANES:(h + 1) * LANES].astype(BF16)
        q_ref[:, h * HEAD_PAD + LANES:(h + 1) * HEAD_PAD] = q_rope[:, h * LANES:(h + 1) * LANES].astype(BF16)

    p1 = _dot(xb, w_ref[:, _C_CKV:_C_MQ])
    ckv = p1[:, 0:KV_LORA]
    ckv = ckv * lax.rsqrt(jnp.mean(ckv * ckv, axis=-1, keepdims=True) + RMS_EPS) * kvg_ref[...]
    ckvb = ckv.astype(BF16)
    k_rope = (p1[:, 256:384] * cos + p1[:, 384:512] * sin).astype(BF16)
    k_nope = _dot(ckvb, wuk_ref[...]).astype(BF16)
    for h in range(MLA_HEADS):
        k_ref[:, h * HEAD_PAD:h * HEAD_PAD + LANES] = k_nope[:, h * LANES:(h + 1) * LANES]
        k_ref[:, h * HEAD_PAD + LANES:(h + 1) * HEAD_PAD] = k_rope
    v_ref[...] = _dot(ckvb, wuv_ref[...]).astype(BF16)

    mq_ref[...] = _dot(xb, w_ref[:, _C_MQ:_C_MK]).astype(BF16)
    mk_ref[...] = (_dot(xb, w_ref[:, _C_MK:_C_MV]) * (ML_QK ** -0.5)).astype(BF16)
    mv_ref[...] = _dot(xb, w_ref[:, _C_MV:_C_MO]).astype(BF16)
    mo_ref[...] = _dot(xb, w_ref[:, _C_MO:PROJ_W]).astype(BF16)
    gt_ref[...] = _nt_dot(wg_ref[...], xb) + gb_ref[...]


def _proj_call(x2, pos_col, invf, w_all, wg_t, gbias, qg, kvg, wuq_all, wuk, wuv):
    T = x2.shape[0]
    tm = PROJ_TM
    row = lambda w: pl.BlockSpec((tm, w), lambda i: (i, 0))
    out_shapes = (
        jax.ShapeDtypeStruct((T, MLA_HEADS * HEAD_PAD), BF16),
        jax.ShapeDtypeStruct((T, MLA_HEADS * HEAD_PAD), BF16),
        jax.ShapeDtypeStruct((T, MLA_OUT), BF16),
        jax.ShapeDtypeStruct((T, ML_HEADS * ML_QK), BF16),
        jax.ShapeDtypeStruct((T, ML_HEADS * ML_QK), BF16),
        jax.ShapeDtypeStruct((T, ML_OUT), BF16),
        jax.ShapeDtypeStruct((T, ML_OUT), BF16),
        jax.ShapeDtypeStruct((8, T), F32),
    )
    return pl.pallas_call(
        _proj_kernel,
        grid=(T // tm,),
        in_specs=[
            row(D_MODEL),
            pl.BlockSpec((tm, 1), lambda i: (i, 0)),
            _const_spec((1, LANES)),
            _const_spec((D_MODEL, PROJ_W)),
            _const_spec((8, D_MODEL)),
            _const_spec((8, 1)),
            _const_spec((1, Q_LORA)),
            _const_spec((1, KV_LORA)),
            _const_spec((Q_LORA, 3072)),
            _const_spec((KV_LORA, MLA_HEADS * MLA_NOPE)),
            _const_spec((KV_LORA, MLA_OUT)),
        ],
        out_specs=(
            row(MLA_HEADS * HEAD_PAD), row(MLA_HEADS * HEAD_PAD), row(MLA_OUT),
            row(ML_HEADS * ML_QK), row(ML_HEADS * ML_QK), row(ML_OUT), row(ML_OUT),
            pl.BlockSpec((8, tm), lambda i: (0, i)),
        ),
        out_shape=out_shapes,
        scratch_shapes=[pltpu.VMEM((tm, D_MODEL), BF16)],
        compiler_params=pltpu.CompilerParams(
            dimension_semantics=("arbitrary",), vmem_limit_bytes=V7X_VMEM_LIMIT_BYTES),
        name="proj_mla_prep",
    )(x2, pos_col, invf, w_all, wg_t, gbias, qg, kvg, wuq_all, wuk, wuv)


def _softmax_update(state, q, k, v, masked):
    m_prev, l_prev, acc = state
    t = q.shape[0]
    s = _nt_dot(q, k)
    if masked:
        r = lax.broadcasted_iota(jnp.int32, (t, t), 0) // CHUNK
        c = lax.broadcasted_iota(jnp.int32, (t, t), 1) // CHUNK
        s = jnp.where(c <= r, s, -jnp.inf)
    m_new = jnp.maximum(m_prev, jnp.max(s, axis=1, keepdims=True))
    alpha = jnp.exp2(m_prev - m_new)
    p = jnp.exp2(s - pltpu.repeat(m_new, t // LANES, axis=1))
    l_new = alpha * l_prev + jnp.sum(p, axis=1, keepdims=True)
    acc_new = alpha * acc + _dot(p.astype(BF16), v)
    return m_new, l_new, acc_new


def _flash_kernel(q_ref, k_ref, v_ref, beta_ref, o_ref, m_sc, l_sc, acc_sc):
    i = pl.program_id(2)
    t = ATT_T
    m_sc[...] = jnp.full(m_sc.shape, NEG_BIG, F32)
    l_sc[...] = jnp.zeros(l_sc.shape, F32)
    acc_sc[...] = jnp.zeros(acc_sc.shape, F32)

    def kv(j):
        off = pl.multiple_of(j * t, t)
        return k_ref[pl.ds(off, t), :], v_ref[pl.ds(off, t), :]

    def load(half):
        return m_sc[half], l_sc[half], acc_sc[half]

    def store(half, state):
        m_sc[half], l_sc[half], acc_sc[half] = state

    def q_half(half):
        return q_ref[half * t:(half + 1) * t, :]

    def body(jj, carry):
        k0, v0 = kv(2 * jj)
        k1, v1 = kv(2 * jj + 1)
        for half in range(2):
            st = _softmax_update(load(half), q_half(half), k0, v0, False)
            store(half, _softmax_update(st, q_half(half), k1, v1, False))
        return carry

    lax.fori_loop(0, i, body, 0)
    k0, v0 = kv(2 * i)
    k1, v1 = kv(2 * i + 1)
    st0 = _softmax_update(load(0), q_half(0), k0, v0, True)
    st1 = _softmax_update(load(1), q_half(1), k0, v0, False)
    st1 = _softmax_update(st1, q_half(1), k1, v1, True)
    for half, (_, l_fin, acc_fin) in enumerate((st0, st1)):
        o_ref[half * t:(half + 1) * t, :] = (acc_fin / l_fin * beta_ref[...]).astype(BF16)


def _flash_call(q_pad, k_pad, v, beta_mla, batch, seq):
    T = q_pad.shape[0]
    t = ATT_T
    tq = 2 * t
    nq = seq // tq
    return pl.pallas_call(
        _flash_kernel,
        grid=(batch, MLA_HEADS, nq),
        in_specs=[
            pl.BlockSpec((tq, HEAD_PAD), lambda b, h, i: (b * nq + i, h)),
            pl.BlockSpec((seq, HEAD_PAD), lambda b, h, i: (b, h)),
            pl.BlockSpec((seq, MLA_V), lambda b, h, i: (b, h)),
            pl.BlockSpec((1, MLA_V), lambda b, h, i: (0, h)),
        ],
        out_specs=pl.BlockSpec((tq, MLA_V), lambda b, h, i: (b * nq + i, h)),
        out_shape=jax.ShapeDtypeStruct((T, MLA_OUT), BF16),
        scratch_shapes=[pltpu.VMEM((2, t, LANES), F32), pltpu.VMEM((2, t, LANES), F32),
                        pltpu.VMEM((2, t, MLA_V), F32)],
        compiler_params=pltpu.CompilerParams(
            dimension_semantics=("arbitrary", "arbitrary", "arbitrary"),
            vmem_limit_bytes=V7X_VMEM_LIMIT_BYTES),
        name="mla_flash",
    )(q_pad, k_pad, v, beta_mla)


def _lane_cumsum(x):
    n = x.shape[-1]
    lane = lax.broadcasted_iota(jnp.int32, x.shape, x.ndim - 1)
    d = 1
    while d < n:
        x = x + jnp.where(lane >= d, pltpu.roll(x, d, x.ndim - 1), 0.0)
        d *= 2
    return x


def _mlstm_kernel(q_ref, k_ref, v_ref, o_ref, gt_ref, hg_ref, beta_ref, out_ref,
                  c_sc, n_sc, m_sc):
    L = ML_L

    @pl.when(pl.program_id(1) == 0)
    def _():
        c_sc[...] = jnp.zeros(c_sc.shape, F32)
        n_sc[...] = jnp.zeros(n_sc.shape, F32)
        m_sc[...] = jnp.full(m_sc.shape, -jnp.inf, F32)

    g8 = gt_ref[...]
    capped = GATE_CAP * jnp.tanh(g8 / GATE_CAP)
    logf = jnp.minimum(capped, 0.0) - jnp.log1p(jnp.exp(-jnp.abs(capped)))
    row8 = lax.broadcasted_iota(jnp.int32, g8.shape, 0)
    bsum = _lane_cumsum(jnp.where(row8 >= ML_HEADS, logf, 0.0))
    gb8 = jnp.where(row8 < ML_HEADS, capped - pltpu.roll(bsum, ML_HEADS, 0), bsum)

    rr = lax.broadcasted_iota(jnp.int32, (L, L), 0)
    cc = lax.broadcasted_iota(jnp.int32, (L, L), 1)
    causal = cc <= rr
    eye = cc == rr

    for h in range(ML_HEADS):
        g_row = gb8[h:h + 1, :]
        b_row = gb8[ML_HEADS + h:ML_HEADS + h + 1, :]
        G = jnp.broadcast_to(g_row, (L, L))
        Bm = jnp.broadcast_to(b_row, (L, L))
        m_prev = m_sc[h:h + 1, 0:1]
        cummax = jnp.max(jnp.where(causal, G, -jnp.inf), axis=1, keepdims=True)
        mcol = jnp.maximum(m_prev, cummax)
        g_col = jnp.sum(jnp.where(eye, G, 0.0), axis=1, keepdims=True)
        b_col = jnp.sum(jnp.where(eye, Bm, 0.0), axis=1, keepdims=True)
        dmat = jnp.where(causal, jnp.exp(G - mcol), 0.0)
        inter_w = jnp.exp(m_prev - mcol)

        qh = q_ref[:, h * ML_QK:(h + 1) * ML_QK]
        kh = k_ref[:, h * ML_QK:(h + 1) * ML_QK]
        vh = v_ref[:, h * ML_V:(h + 1) * ML_V]
        ct = c_sc[h]
        nrow = n_sc[h:h + 1, :]

        s = _nt_dot(qh, kh) * dmat
        num = _dot(s.astype(BF16), vh) + inter_w * _dot(qh, ct.astype(BF16))
        den = (jnp.sum(s, axis=1, keepdims=True)
               + inter_w * jnp.sum(qh.astype(F32) * nrow, axis=1, keepdims=True))
        hval = num / jnp.maximum(jnp.abs(den), jnp.exp(-(b_col + mcol)))

        m_last = mcol[L - 1:L, :]
        decay = inter_w[L - 1:L, :]
        w_col = jnp.exp(g_col - m_last)
        kw = kh.astype(F32) * w_col
        upd = lax.dot_general(kw.astype(BF16), vh, (((0,), (0,)), ((), ())),
                              preferred_element_type=F32)
        c_sc[h] = decay * ct + upd
        n_sc[h:h + 1, :] = decay * nrow + jnp.sum(kw, axis=0, keepdims=True)
        m_sc[h:h + 1, :] = jnp.broadcast_to(b_row[:, L - 1:L] + m_last, (1, LANES))

        hn = hval * lax.rsqrt(jnp.mean(hval * hval, axis=-1, keepdims=True) + RMS_EPS)
        hn = hn * hg_ref[:, h * ML_V:(h + 1) * ML_V]
        og = jax.nn.sigmoid(o_ref[:, h * ML_V:(h + 1) * ML_V].astype(F32))
        out_ref[:, h * ML_V:(h + 1) * ML_V] = (
            og * hn * beta_ref[:, h * ML_V:(h + 1) * ML_V]).astype(BF16)


def _mlstm_call(mq, mk, mv, mo, gates_t, head_g, beta_ml, batch, seq):
    T = mq.shape[0]
    L = ML_L
    nc = seq // L
    blk = lambda w: pl.BlockSpec((L, w), lambda b, c: (b * nc + c, 0))
    return pl.pallas_call(
        _mlstm_kernel,
        grid=(batch, nc),
        in_specs=[
            blk(ML_HEADS * ML_QK), blk(ML_HEADS * ML_QK), blk(ML_OUT), blk(ML_OUT),
            pl.BlockSpec((8, L), lambda b, c: (0, b * nc + c)),
            _const_spec((1, ML_OUT)),
            _const_spec((1, ML_OUT)),
        ],
        out_specs=blk(ML_OUT),
        out_shape=jax.ShapeDtypeStruct((T, ML_OUT), BF16),
        scratch_shapes=[pltpu.VMEM((ML_HEADS, ML_QK, ML_V), F32),
                        pltpu.VMEM((8, ML_QK), F32),
                        pltpu.VMEM((8, LANES), F32)],
        compiler_params=pltpu.CompilerParams(
            dimension_semantics=("arbitrary", "arbitrary"),
            vmem_limit_bytes=V7X_VMEM_LIMIT_BYTES),
        name="mlstm_scan",
    )(mq, mk, mv, mo, gates_t, head_g, beta_ml)


def _layernorm(y, g, b):
    mu = jnp.mean(y, axis=-1, keepdims=True)
    yc = y - mu
    return yc * lax.rsqrt(jnp.mean(yc * yc, axis=-1, keepdims=True) + LN_EPS) * g + b


def _outproj_kernel(a_ref, m_ref, x_ref, w_ref, g_ref, b_ref, o_ref):
    acc = _dot(a_ref[...], w_ref[0:MLA_OUT, :]) + _dot(m_ref[...], w_ref[MLA_OUT:, :])
    o_ref[...] = _layernorm(DN_ALPHA * x_ref[...] + acc, g_ref[...], b_ref[...])


def _outproj_call(attn, mem, x2, w_out, g, b):
    T = x2.shape[0]
    tm = OUT_TM
    return pl.pallas_call(
        _outproj_kernel,
        grid=(T // tm,),
        in_specs=[
            pl.BlockSpec((tm, MLA_OUT), lambda i: (i, 0)),
            pl.BlockSpec((tm, ML_OUT), lambda i: (i, 0)),
            pl.BlockSpec((tm, D_MODEL), lambda i: (i, 0)),
            _const_spec((MLA_OUT + ML_OUT, D_MODEL)),
            _const_spec((1, D_MODEL)),
            _const_spec((1, D_MODEL)),
        ],
        out_specs=pl.BlockSpec((tm, D_MODEL), lambda i: (i, 0)),
        out_shape=jax.ShapeDtypeStruct((T, D_MODEL), F32),
        compiler_params=pltpu.CompilerParams(
            dimension_semantics=("arbitrary",), vmem_limit_bytes=V7X_VMEM_LIMIT_BYTES),
        name="outproj_ln",
    )(attn, mem, x2, w_out, g, b)


def _gelu_tanh(x):
    c = math.sqrt(2.0 / math.pi)
    return 0.5 * x * (1.0 + jnp.tanh(c * (x + 0.044715 * (x * x * x))))


def _ffn_kernel(tiles_per_seq, h_ref, wg_ref, wv_ref, wd_ref, cw_ref, cb_ref, g_ref, b_ref,
                o_ref, hb_ref, gbuf_ref, halo_ref):
    i = pl.program_id(0)
    f = pl.program_id(1)
    nf = pl.num_programs(1)
    tm = FFN_TM

    @pl.when(f == 0)
    def _():
        hb_ref[...] = h_ref[...].astype(BF16)
        o_ref[...] = DN_ALPHA * h_ref[...]

    hb = hb_ref[...]
    gate = _dot(hb, wg_ref[...])
    val = _dot(hb, wv_ref[...])

    first = (i % tiles_per_seq) == 0

    @pl.when(first)
    def _():
        gbuf_ref[0:8, :] = jnp.zeros((8, gate.shape[1]), F32)

    @pl.when(jnp.logical_not(first))
    def _():
        gbuf_ref[0:8, :] = halo_ref[f]

    gbuf_ref[8:8 + tm, :] = gate
    halo_ref[f] = gate[tm - 8:tm, :]
    conv = (cw_ref[0:1, :] * gbuf_ref[6:6 + tm, :] + cw_ref[1:2, :] * gbuf_ref[7:7 + tm, :]
            + cw_ref[2:3, :] * gate + cb_ref[...])
    act = (_gelu_tanh(conv) * val).astype(BF16)
    o_ref[...] += _dot(act, wd_ref[...])

    @pl.when(f == nf - 1)
    def _():
        o_ref[...] = _layernorm(o_ref[...], g_ref[...], b_ref[...])


def _ffn_call(h1, wg, wv, wd, conv_w, conv_b, g, b, seq):
    T = h1.shape[0]
    tm, tf = FFN_TM, FFN_TF
    nf = D_FF // tf
    kern = functools.partial(_ffn_kernel, seq // tm)
    return pl.pallas_call(
        kern,
        grid=(T // tm, nf),
        in_specs=[
            pl.BlockSpec((tm, D_MODEL), lambda i, f: (i, 0)),
            pl.BlockSpec((D_MODEL, tf), lambda i, f: (0, f)),
            pl.BlockSpec((D_MODEL, tf), lambda i, f: (0, f)),
            pl.BlockSpec((tf, D_MODEL), lambda i, f: (f, 0)),
            pl.BlockSpec((3, tf), lambda i, f: (0, f)),
            pl.BlockSpec((1, tf), lambda i, f: (0, f)),
            _const_spec((1, D_MODEL)),
            _const_spec((1, D_MODEL)),
        ],
        out_specs=pl.BlockSpec((tm, D_MODEL), lambda i, f: (i, 0)),
        out_shape=jax.ShapeDtypeStruct((T, D_MODEL), F32),
        scratch_shapes=[pltpu.VMEM((tm, D_MODEL), BF16),
                        pltpu.VMEM((tm + 8, tf), F32),
                        pltpu.VMEM((nf, 8, tf), F32)],
        compiler_params=pltpu.CompilerParams(
            dimension_semantics=("arbitrary", "arbitrary"),
            vmem_limit_bytes=V7X_VMEM_LIMIT_BYTES),
        name="convglu_ffn_ln",
    )(h1, wg, wv, wd, conv_w, conv_b, g, b)


def _prep_proj_weights(w_in, w_uq, b_igate, b_fgate):
    half = MLA_ROPE // 2
    o = 0
    w_cq = w_in[:, o:o + Q_LORA]; o += Q_LORA
    w_ckv = w_in[:, o:o + KV_LORA]; o += KV_LORA
    w_kr = w_in[:, o:o + MLA_ROPE]; o += MLA_ROPE
    w_mq = w_in[:, o:o + ML_HEADS * ML_QK]; o += ML_HEADS * ML_QK
    w_mk = w_in[:, o:o + ML_HEADS * ML_QK]; o += ML_HEADS * ML_QK
    w_mv = w_in[:, o:o + ML_OUT]; o += ML_OUT
    w_mo = w_in[:, o:o + ML_OUT]; o += ML_OUT
    w_gates = w_in[:, o:o + 2 * ML_HEADS]
    w_krot = jnp.concatenate([-w_kr[:, half:], w_kr[:, :half]], axis=1)
    z64 = jnp.zeros((D_MODEL, LANES - MLA_ROPE), w_in.dtype)
    w_all = jnp.concatenate([w_cq, w_ckv, w_kr, z64, w_krot, z64, w_mq, w_mk, w_mv, w_mo], axis=1)
    wg_t = w_gates.T
    gbias = jnp.concatenate([b_igate, b_fgate])[:, None]

    uq = w_uq.reshape(Q_LORA, MLA_HEADS, MLA_NOPE + MLA_ROPE)
    uq_nope = uq[:, :, :MLA_NOPE].reshape(Q_LORA, MLA_HEADS * MLA_NOPE)
    uq_r = uq[:, :, MLA_NOPE:]
    uq_rot = jnp.concatenate([-uq_r[:, :, half:], uq_r[:, :, :half]], axis=2)
    zpad = jnp.zeros((Q_LORA, MLA_HEADS, LANES - MLA_ROPE), w_uq.dtype)
    uq_r = jnp.concatenate([uq_r, zpad], axis=2).reshape(Q_LORA, MLA_HEADS * LANES)
    uq_rot = jnp.concatenate([uq_rot, zpad], axis=2).reshape(Q_LORA, MLA_HEADS * LANES)
    wuq_all = jnp.concatenate([uq_nope, uq_r, uq_rot], axis=1)
    return w_all.astype(BF16), wg_t.astype(BF16), gbias.astype(F32), wuq_all.astype(BF16)


def kernel(x, positions, w_in, q_norm_g, kv_norm_g, w_uq, w_uk, w_uv, b_igate, b_fgate,
           ml_head_g, beta_mla, beta_ml, w_out, ln1_g, ln1_b, w_ffn_gate, w_ffn_val,
           conv_w, conv_b, w_down, ln2_g, ln2_b):
    B, S, D = x.shape
    T = B * S
    assert D == D_MODEL and w_in.shape[0] == DEPTH == 1
    assert S % (2 * ATT_T) == 0 and S % ML_L == 0 and S % FFN_TM == 0 and T % PROJ_TM == 0
    l = 0
    x2 = x.reshape(T, D)
    pos_col = positions.reshape(T, 1).astype(F32)
    inv_freq = 1.0 / (ROPE_BASE ** (jnp.arange(0, MLA_ROPE, 2, dtype=F32) / MLA_ROPE))
    invf = jnp.tile(inv_freq, LANES // (MLA_ROPE // 2))[None, :]

    w_all, wg_t, gbias, wuq_all = _prep_proj_weights(w_in[l], w_uq[l], b_igate[l], b_fgate[l])
    q_pad, k_pad, v, mq, mk, mv, mo, gates_t = _proj_call(
        x2, pos_col, invf, w_all, wg_t, gbias,
        q_norm_g[l][None, :], kv_norm_g[l][None, :], wuq_all,
        w_uk[l].astype(BF16), w_uv[l].astype(BF16))

    attn = _flash_call(q_pad, k_pad, v, beta_mla[l][None, :], B, S)
    mem = _mlstm_call(mq, mk, mv, mo, gates_t, ml_head_g[l][None, :], beta_ml[l][None, :], B, S)
    h1 = _outproj_call(attn, mem, x2, w_out[l].astype(BF16), ln1_g[l][None, :], ln1_b[l][None, :])
    out = _ffn_call(h1, w_ffn_gate[l].astype(BF16), w_ffn_val[l].astype(BF16),
                    w_down[l].astype(BF16), conv_w[l], conv_b[l][None, :],
                    ln2_g[l][None, :], ln2_b[l][None, :], S)
    return out.reshape(B, S, D)
```

```python
import functools
import math

import jax
import jax.numpy as jnp
from jax import lax
from jax.experimental import pallas as pl
from jax.experimental.pallas import tpu as pltpu

D_MODEL = 2048
CHUNK = 64
MLA_HEADS = 8
MLA_NOPE = 128
MLA_ROPE = 64
MLA_V = 128
Q_LORA = 512
KV_LORA = 256
ROPE_BASE = 10000.0
MLA_OUT = MLA_HEADS * MLA_V
ML_HEADS = 4
ML_QK = 128
ML_V = 256
ML_OUT = ML_HEADS * ML_V
GATE_CAP = 15.0
D_FF = 5632
RMS_EPS = 1e-6
LN_EPS = 1e-5
DEPTH = 1
DN_ALPHA = (2 * DEPTH) ** 0.25

LANES = 128
V7X_VMEM_LIMIT_BYTES = 58 * 1024 * 1024

PROJ_TM = 256
ATT_T = 512
ML_L = 256
OUT_TM = 512
OUT_SUBTILES = 4
FFN_TM = 512
FFN_TF = 512
FFN_SUB = 256

HEAD_PAD = 2 * LANES
PROJ_W = 4096
_C_CQ, _C_CKV, _C_KR, _C_KROT, _C_MQ, _C_MK, _C_MV, _C_MO = 0, 512, 768, 896, 1024, 1536, 2048, 3072

F32 = jnp.float32
BF16 = jnp.bfloat16
NEG_BIG = -1e30


def _nt_dot(a, b):
    return lax.dot_general(a, b, (((1,), (1,)), ((), ())), preferred_element_type=F32)


def _dot(a, b):
    return jnp.dot(a, b, preferred_element_type=F32)


def _const_spec(shape):
    nd = len(shape)
    return pl.BlockSpec(shape, lambda *_: (0,) * nd, pipeline_mode=pl.Buffered(1))


def _proj_kernel(x_ref, pos_ref, invf_ref, w_ref, wg_ref, gb_ref, qg_ref, kvg_ref,
                 wuq_ref, wuk_ref, wuv_ref,
                 q_ref, k_ref, vt_ref, mq_ref, mk_ref, mv_ref, mo_ref, gt_ref,
                 xb_ref):
    xb_ref[...] = x_ref[...].astype(BF16)
    xb = xb_ref[...]
    tm = xb.shape[0]

    cq = _dot(xb, w_ref[:, _C_CQ:_C_CQ + Q_LORA])
    p1 = _dot(xb, w_ref[:, _C_CKV:_C_MQ])
    mq_ref[...] = _dot(xb, w_ref[:, _C_MQ:_C_MK]).astype(BF16)
    mk_ref[...] = (_dot(xb, w_ref[:, _C_MK:_C_MV]) * (ML_QK ** -0.5)).astype(BF16)

    ang = pos_ref[...] * invf_ref[...]
    cos = jnp.cos(ang)
    sin = jnp.sin(ang)

    cq = cq * lax.rsqrt(jnp.mean(cq * cq, axis=-1, keepdims=True) + RMS_EPS) * qg_ref[...]
    cqb = cq.astype(BF16)
    ckv = p1[:, 0:KV_LORA]
    ckv = ckv * lax.rsqrt(jnp.mean(ckv * ckv, axis=-1, keepdims=True) + RMS_EPS) * kvg_ref[...]
    ckvb = ckv.astype(BF16)
    k_rope = (p1[:, 256:384] * cos + p1[:, 384:512] * sin).astype(BF16)

    qscale = (MLA_NOPE + MLA_ROPE) ** -0.5 * math.log2(math.e)
    q_nope = _dot(cqb, wuq_ref[:, 0:1024]) * qscale
    q_r = _dot(cqb, wuq_ref[:, 1024:2048])
    q_rot = _dot(cqb, wuq_ref[:, 2048:3072])
    k_nope = _dot(ckvb, wuk_ref[...]).astype(BF16)
    vt_ref[0] = _nt_dot(wuv_ref[...], ckvb).astype(BF16)
    mv_ref[...] = _dot(xb, w_ref[:, _C_MV:_C_MO]).astype(BF16)
    mo_ref[...] = _dot(xb, w_ref[:, _C_MO:PROJ_W]).astype(BF16)
    gt_ref[...] = _nt_dot(wg_ref[...], xb) + gb_ref[...]

    cos8 = jnp.concatenate([cos] * MLA_HEADS, axis=1)
    sin8 = jnp.concatenate([sin] * MLA_HEADS, axis=1)
    q_rope = (q_r * cos8 + q_rot * sin8) * qscale
    for h in range(MLA_HEADS):
        q_ref[:, h * HEAD_PAD:h * HEAD_PAD + LANES] = q_nope[:, h * LANES:(h + 1) * LANES].astype(BF16)
        q_ref[:, h * HEAD_PAD + LANES:(h + 1) * HEAD_PAD] = q_rope[:, h * LANES:(h + 1) * LANES].astype(BF16)
        k_ref[:, h * HEAD_PAD:h * HEAD_PAD + LANES] = k_nope[:, h * LANES:(h + 1) * LANES]
        k_ref[:, h * HEAD_PAD + LANES:(h + 1) * HEAD_PAD] = k_rope


def _proj_call(x2, pos_col, invf, w_all, wg_t, gbias, qg, kvg, wuq_all, wuk, wuv):
    T = x2.shape[0]
    tm = PROJ_TM
    row = lambda w: pl.BlockSpec((tm, w), lambda i: (i, 0))
    out_shapes = (
        jax.ShapeDtypeStruct((T, MLA_HEADS * HEAD_PAD), BF16),
        jax.ShapeDtypeStruct((T, MLA_HEADS * HEAD_PAD), BF16),
        jax.ShapeDtypeStruct((T // ATT_T, MLA_OUT, ATT_T), BF16),
        jax.ShapeDtypeStruct((T, ML_HEADS * ML_QK), BF16),
        jax.ShapeDtypeStruct((T, ML_HEADS * ML_QK), BF16),
        jax.ShapeDtypeStruct((T, ML_OUT), BF16),
        jax.ShapeDtypeStruct((T, ML_OUT), BF16),
        jax.ShapeDtypeStruct((8, T), F32),
    )
    return pl.pallas_call(
        _proj_kernel,
        grid=(T // tm,),
        in_specs=[
            row(D_MODEL),
            pl.BlockSpec((tm, 1), lambda i: (i, 0)),
            _const_spec((1, LANES)),
            _const_spec((D_MODEL, PROJ_W)),
            _const_spec((8, D_MODEL)),
            _const_spec((8, 1)),
            _const_spec((1, Q_LORA)),
            _const_spec((1, KV_LORA)),
            _const_spec((Q_LORA, 3072)),
            _const_spec((KV_LORA, MLA_HEADS * MLA_NOPE)),
            _const_spec((MLA_OUT, KV_LORA)),
        ],
        out_specs=(
            row(MLA_HEADS * HEAD_PAD), row(MLA_HEADS * HEAD_PAD),
            pl.BlockSpec((1, MLA_OUT, tm), lambda i: (i // (ATT_T // tm), 0, i % (ATT_T // tm))),
            row(ML_HEADS * ML_QK), row(ML_HEADS * ML_QK), row(ML_OUT), row(ML_OUT),
            pl.BlockSpec((8, tm), lambda i: (0, i)),
        ),
        out_shape=out_shapes,
        scratch_shapes=[pltpu.VMEM((tm, D_MODEL), BF16)],
        compiler_params=pltpu.CompilerParams(
            dimension_semantics=("arbitrary",), vmem_limit_bytes=V7X_VMEM_LIMIT_BYTES),
        name="proj_mla_prep",
    )(x2, pos_col, invf, w_all, wg_t, gbias, qg, kvg, wuq_all, wuk, wuv)


def _scores(q, k):
    return _nt_dot(k, q)


def _chunk_causal(s):
    t = s.shape[0]
    kc = lax.broadcasted_iota(jnp.int32, (t, t), 0) // CHUNK
    qc = lax.broadcasted_iota(jnp.int32, (t, t), 1) // CHUNK
    return jnp.where(kc <= qc, s, -jnp.inf)


def _softmax_update(state, s, vt):
    m_prev, l_prev, acc = state
    m_new = jnp.maximum(m_prev, jnp.max(s, axis=0, keepdims=True))
    alpha = jnp.exp2(m_prev - m_new)
    p = jnp.exp2(s - m_new)
    l_new = alpha * l_prev + jnp.sum(p, axis=0, keepdims=True)
    acc_new = alpha * acc + _dot(vt, p.astype(BF16))
    return m_new, l_new, acc_new


def _flash_kernel(q_ref, k_ref, vt_ref, beta_ref, o_ref, m_sc, l_sc, acc_sc, s_sc):
    i = pl.program_id(2)
    t = ATT_T
    m_sc[...] = jnp.full(m_sc.shape, NEG_BIG, F32)
    l_sc[...] = jnp.zeros(l_sc.shape, F32)
    acc_sc[...] = jnp.zeros(acc_sc.shape, F32)

    def keys(j):
        return k_ref[pl.ds(pl.multiple_of(j * t, t), t), :]

    def load(half):
        return m_sc[half], l_sc[half], acc_sc[half]

    def store(half, state):
        m_sc[half], l_sc[half], acc_sc[half] = state

    q0 = q_ref[0:t, :]
    q1 = q_ref[t:2 * t, :]

    kn0, kn1 = keys(0), keys(1)
    s_sc[0] = _scores(q0, kn0)
    s_sc[1] = _scores(q1, kn0)
    s_sc[2] = _scores(q0, kn1)
    s_sc[3] = _scores(q1, kn1)

    def body(jj, carry):
        v0, v1 = vt_ref[2 * jj], vt_ref[2 * jj + 1]
        kn0, kn1 = keys(2 * jj + 2), keys(2 * jj + 3)
        sn0 = _scores(q0, kn0)
        sn1 = _scores(q1, kn0)
        st0 = _softmax_update(load(0), s_sc[0], v0)
        s_sc[0] = sn0
        st1 = _softmax_update(load(1), s_sc[1], v0)
        s_sc[1] = sn1
        sn2 = _scores(q0, kn1)
        store(0, _softmax_update(st0, s_sc[2], v1))
        s_sc[2] = sn2
        store(1, _softmax_update(st1, s_sc[3], v1))
        s_sc[3] = _scores(q1, kn1)
        return carry

    lax.fori_loop(0, i, body, 0)
    v0, v1 = vt_ref[2 * i], vt_ref[2 * i + 1]
    st0 = _softmax_update(load(0), _chunk_causal(s_sc[0]), v0)
    st1 = _softmax_update(load(1), s_sc[1], v0)
    st1 = _softmax_update(st1, _chunk_causal(s_sc[3]), v1)
    for half, (_, l_fin, acc_fin) in enumerate((st0, st1)):
        out = jnp.transpose(acc_fin / l_fin) * beta_ref[...]
        o_ref[half * t:(half + 1) * t, :] = out.astype(BF16)


def _flash_call(q_pad, k_pad, vt, beta_mla, batch, seq):
    T = q_pad.shape[0]
    t = ATT_T
    tq = 2 * t
    nq = seq // tq
    nkv = seq // t
    return pl.pallas_call(
        _flash_kernel,
        grid=(batch, MLA_HEADS, nq),
        in_specs=[
            pl.BlockSpec((tq, HEAD_PAD), lambda b, h, i: (b * nq + i, h)),
            pl.BlockSpec((seq, HEAD_PAD), lambda b, h, i: (b, h)),
            pl.BlockSpec((nkv, MLA_V, t), lambda b, h, i: (b, h, 0)),
            pl.BlockSpec((1, MLA_V), lambda b, h, i: (0, h)),
        ],
        out_specs=pl.BlockSpec((tq, MLA_V), lambda b, h, i: (b * nq + i, h)),
        out_shape=jax.ShapeDtypeStruct((T, MLA_OUT), BF16),
        scratch_shapes=[pltpu.VMEM((2, 1, t), F32), pltpu.VMEM((2, 1, t), F32),
                        pltpu.VMEM((2, MLA_V, t), F32), pltpu.VMEM((4, t, t), F32)],
        compiler_params=pltpu.CompilerParams(
            dimension_semantics=("arbitrary", "arbitrary", "arbitrary"),
            vmem_limit_bytes=V7X_VMEM_LIMIT_BYTES),
        name="mla_flash",
    )(q_pad, k_pad, vt, beta_mla)


def _lane_cumsum(x):
    n = x.shape[-1]
    lane = lax.broadcasted_iota(jnp.int32, x.shape, x.ndim - 1)
    d = 1
    while d < n:
        x = x + jnp.where(lane >= d, pltpu.roll(x, d, x.ndim - 1), 0.0)
        d *= 2
    return x


def _mlstm_kernel(q_ref, k_ref, v_ref, o_ref, gt_ref, hg_ref, beta_ref, out_ref,
                  c_sc, n_sc, m_sc):
    L = ML_L

    @pl.when(pl.program_id(1) == 0)
    def _():
        c_sc[...] = jnp.zeros(c_sc.shape, F32)
        n_sc[...] = jnp.zeros(n_sc.shape, F32)
        m_sc[...] = jnp.full(m_sc.shape, -jnp.inf, F32)

    g8 = gt_ref[...]
    capped = GATE_CAP * jnp.tanh(g8 / GATE_CAP)
    logf = jnp.minimum(capped, 0.0) - jnp.log1p(jnp.exp(-jnp.abs(capped)))
    row8 = lax.broadcasted_iota(jnp.int32, g8.shape, 0)
    bsum = _lane_cumsum(jnp.where(row8 >= ML_HEADS, logf, 0.0))
    gb8 = jnp.where(row8 < ML_HEADS, capped - pltpu.roll(bsum, ML_HEADS, 0), bsum)

    rr = lax.broadcasted_iota(jnp.int32, (L, L), 0)
    cc = lax.broadcasted_iota(jnp.int32, (L, L), 1)
    causal = cc <= rr
    eye = cc == rr

    for h in range(ML_HEADS):
        g_row = gb8[h:h + 1, :]
        b_row = gb8[ML_HEADS + h:ML_HEADS + h + 1, :]
        G = jnp.broadcast_to(g_row, (L, L))
        Bm = jnp.broadcast_to(b_row, (L, L))
        m_prev = m_sc[h:h + 1, 0:1]
        cummax = jnp.max(jnp.where(causal, G, -jnp.inf), axis=1, keepdims=True)
        mcol = jnp.maximum(m_prev, cummax)
        g_col = jnp.sum(jnp.where(eye, G, 0.0), axis=1, keepdims=True)
        b_col = jnp.sum(jnp.where(eye, Bm, 0.0), axis=1, keepdims=True)
        dmat = jnp.where(causal, jnp.exp(G - mcol), 0.0)
        inter_w = jnp.exp(m_prev - mcol)

        qh = q_ref[:, h * ML_QK:(h + 1) * ML_QK]
        kh = k_ref[:, h * ML_QK:(h + 1) * ML_QK]
        vh = v_ref[:, h * ML_V:(h + 1) * ML_V]
        ct = c_sc[h]
        nrow = n_sc[h:h + 1, :]

        s = _nt_dot(qh, kh) * dmat
        num = _dot(s.astype(BF16), vh) + inter_w * _dot(qh, ct.astype(BF16))
        den = (jnp.sum(s, axis=1, keepdims=True)
               + inter_w * jnp.sum(qh.astype(F32) * nrow, axis=1, keepdims=True))
        hval = num / jnp.maximum(jnp.abs(den), jnp.exp(-(b_col + mcol)))

        m_last = mcol[L - 1:L, :]
        decay = inter_w[L - 1:L, :]
        w_col = jnp.exp(g_col - m_last)
        kw = kh.astype(F32) * w_col
        upd = lax.dot_general(kw.astype(BF16), vh, (((0,), (0,)), ((), ())),
                              preferred_element_type=F32)
        c_sc[h] = decay * ct + upd
        n_sc[h:h + 1, :] = decay * nrow + jnp.sum(kw, axis=0, keepdims=True)
        m_sc[h:h + 1, :] = jnp.broadcast_to(b_row[:, L - 1:L] + m_last, (1, LANES))

        hn = hval * lax.rsqrt(jnp.mean(hval * hval, axis=-1, keepdims=True) + RMS_EPS)
        hn = hn * hg_ref[:, h * ML_V:(h + 1) * ML_V]
        og = jax.nn.sigmoid(o_ref[:, h * ML_V:(h + 1) * ML_V].astype(F32))
        out_ref[:, h * ML_V:(h + 1) * ML_V] = (
            og * hn * beta_ref[:, h * ML_V:(h + 1) * ML_V]).astype(BF16)


def _mlstm_call(mq, mk, mv, mo, gates_t, head_g, beta_ml, batch, seq):
    T = mq.shape[0]
    L = ML_L
    nc = seq // L
    blk = lambda w: pl.BlockSpec((L, w), lambda b, c: (b * nc + c, 0))
    return pl.pallas_call(
        _mlstm_kernel,
        grid=(batch, nc),
        in_specs=[
            blk(ML_HEADS * ML_QK), blk(ML_HEADS * ML_QK), blk(ML_OUT), blk(ML_OUT),
            pl.BlockSpec((8, L), lambda b, c: (0, b * nc + c)),
            _const_spec((1, ML_OUT)),
            _const_spec((1, ML_OUT)),
        ],
        out_specs=blk(ML_OUT),
        out_shape=jax.ShapeDtypeStruct((T, ML_OUT), BF16),
        scratch_shapes=[pltpu.VMEM((ML_HEADS, ML_QK, ML_V), F32),
                        pltpu.VMEM((8, ML_QK), F32),
                        pltpu.VMEM((8, LANES), F32)],
        compiler_params=pltpu.CompilerParams(
            dimension_semantics=("arbitrary", "arbitrary"),
            vmem_limit_bytes=V7X_VMEM_LIMIT_BYTES),
        name="mlstm_scan",
    )(mq, mk, mv, mo, gates_t, head_g, beta_ml)


def _layernorm(y, g, b):
    mu = jnp.mean(y, axis=-1, keepdims=True)
    yc = y - mu
    return yc * lax.rsqrt(jnp.mean(yc * yc, axis=-1, keepdims=True) + LN_EPS) * g + b


def _outproj_kernel(a_ref, m_ref, x_ref, w_ref, g_ref, b_ref, o_ref):
    tm = o_ref.shape[0]
    sub = tm // OUT_SUBTILES

    def rows(r):
        return slice(r * sub, (r + 1) * sub)

    def project(r):
        return _dot(a_ref[rows(r), :], w_ref[0:MLA_OUT, :]) + _dot(m_ref[rows(r), :], w_ref[MLA_OUT:, :])

    def finish(r, acc):
        o_ref[rows(r), :] = _layernorm(DN_ALPHA * x_ref[rows(r), :] + acc, g_ref[...], b_ref[...])

    acc = project(0)
    for r in range(1, OUT_SUBTILES):
        nxt = project(r)
        finish(r - 1, acc)
        acc = nxt
    finish(OUT_SUBTILES - 1, acc)


def _outproj_call(attn, mem, x2, w_out, g, b):
    T = x2.shape[0]
    tm = OUT_TM
    return pl.pallas_call(
        _outproj_kernel,
        grid=(T // tm,),
        in_specs=[
            pl.BlockSpec((tm, MLA_OUT), lambda i: (i, 0)),
            pl.BlockSpec((tm, ML_OUT), lambda i: (i, 0)),
            pl.BlockSpec((tm, D_MODEL), lambda i: (i, 0)),
            _const_spec((MLA_OUT + ML_OUT, D_MODEL)),
            _const_spec((1, D_MODEL)),
            _const_spec((1, D_MODEL)),
        ],
        out_specs=pl.BlockSpec((tm, D_MODEL), lambda i: (i, 0)),
        out_shape=jax.ShapeDtypeStruct((T, D_MODEL), F32),
        compiler_params=pltpu.CompilerParams(
            dimension_semantics=("arbitrary",), vmem_limit_bytes=V7X_VMEM_LIMIT_BYTES),
        name="outproj_ln",
    )(attn, mem, x2, w_out, g, b)


def _gelu_tanh(x):
    c = math.sqrt(2.0 / math.pi)
    return 0.5 * x * (1.0 + jnp.tanh(c * (x + 0.044715 * (x * x * x))))


def _ffn_kernel(tiles_per_seq, h_ref, wg_ref, wv_ref, wd_ref, cw_ref, cb_ref, g_ref, b_ref,
                o_ref, hb_ref, gbuf_ref, halo_ref):
    i = pl.program_id(0)
    f = pl.program_id(1)
    nf = pl.num_programs(1)
    tm = FFN_TM

    @pl.when(f == 0)
    def _():
        hb_ref[...] = h_ref[...].astype(BF16)
        o_ref[...] = DN_ALPHA * h_ref[...]

    @pl.when(jnp.logical_and(i == 0, f == 0))
    def _():
        halo_ref[...] = jnp.zeros(halo_ref.shape, F32)

    hb = hb_ref[...]
    first = (i % tiles_per_seq) == 0
    prev = jnp.where(first, 0.0, halo_ref[f])

    nsub = FFN_TF // FFN_SUB
    cols = [slice(c * FFN_SUB, (c + 1) * FFN_SUB) for c in range(nsub)]
    gates = []
    vals = []
    for c in cols:
        gates.append(_dot(hb, wg_ref[:, c]))
        vals.append(_dot(hb, wv_ref[:, c]))
    acc = None
    for c, gate, val in zip(cols, gates, vals):
        gbuf_ref[0:8, c] = prev[:, c]
        gbuf_ref[8:8 + tm, c] = gate
        halo_ref[f, :, c] = gate[tm - 8:tm, :]
        conv = (cw_ref[0:1, c] * gbuf_ref[6:6 + tm, c] + cw_ref[1:2, c] * gbuf_ref[7:7 + tm, c]
                + cw_ref[2:3, c] * gate + cb_ref[:, c])
        act = (_gelu_tanh(conv) * val).astype(BF16)
        part = _dot(act, wd_ref[c, :])
        acc = part if acc is None else acc + part
    o_ref[...] += acc

    @pl.when(f == nf - 1)
    def _():
        o_ref[...] = _layernorm(o_ref[...], g_ref[...], b_ref[...])


def _ffn_call(h1, wg, wv, wd, conv_w, conv_b, g, b, seq):
    T = h1.shape[0]
    tm, tf = FFN_TM, FFN_TF
    nf = D_FF // tf
    kern = functools.partial(_ffn_kernel, seq // tm)
    return pl.pallas_call(
        kern,
        grid=(T // tm, nf),
        in_specs=[
            pl.BlockSpec((tm, D_MODEL), lambda i, f: (i, 0)),
            pl.BlockSpec((D_MODEL, tf), lambda i, f: (0, f)),
            pl.BlockSpec((D_MODEL, tf), lambda i, f: (0, f)),
            pl.BlockSpec((tf, D_MODEL), lambda i, f: (f, 0)),
            pl.BlockSpec((3, tf), lambda i, f: (0, f)),
            pl.BlockSpec((1, tf), lambda i, f: (0, f)),
            _const_spec((1, D_MODEL)),
            _const_spec((1, D_MODEL)),
        ],
        out_specs=pl.BlockSpec((tm, D_MODEL), lambda i, f: (i, 0)),
        out_shape=jax.ShapeDtypeStruct((T, D_MODEL), F32),
        scratch_shapes=[pltpu.VMEM((tm, D_MODEL), BF16),
                        pltpu.VMEM((tm + 8, tf), F32),
                        pltpu.VMEM((nf, 8, tf), F32)],
        compiler_params=pltpu.CompilerParams(
            dimension_semantics=("arbitrary", "arbitrary"),
            vmem_limit_bytes=V7X_VMEM_LIMIT_BYTES),
        name="convglu_ffn_ln",
    )(h1, wg, wv, wd, conv_w, conv_b, g, b)


def _prep_proj_weights(w_in, w_uq, b_igate, b_fgate):
    half = MLA_ROPE // 2
    o = 0
    w_cq = w_in[:, o:o + Q_LORA]; o += Q_LORA
    w_ckv = w_in[:, o:o + KV_LORA]; o += KV_LORA
    w_kr = w_in[:, o:o + MLA_ROPE]; o += MLA_ROPE
    w_mq = w_in[:, o:o + ML_HEADS * ML_QK]; o += ML_HEADS * ML_QK
    w_mk = w_in[:, o:o + ML_HEADS * ML_QK]; o += ML_HEADS * ML_QK
    w_mv = w_in[:, o:o + ML_OUT]; o += ML_OUT
    w_mo = w_in[:, o:o + ML_OUT]; o += ML_OUT
    w_gates = w_in[:, o:o + 2 * ML_HEADS]
    w_krot = jnp.concatenate([-w_kr[:, half:], w_kr[:, :half]], axis=1)
    z64 = jnp.zeros((D_MODEL, LANES - MLA_ROPE), w_in.dtype)
    w_all = jnp.concatenate([w_cq, w_ckv, w_kr, z64, w_krot, z64, w_mq, w_mk, w_mv, w_mo], axis=1)
    wg_t = w_gates.T
    gbias = jnp.concatenate([b_igate, b_fgate])[:, None]

    uq = w_uq.reshape(Q_LORA, MLA_HEADS, MLA_NOPE + MLA_ROPE)
    uq_nope = uq[:, :, :MLA_NOPE].reshape(Q_LORA, MLA_HEADS * MLA_NOPE)
    uq_r = uq[:, :, MLA_NOPE:]
    uq_rot = jnp.concatenate([-uq_r[:, :, half:], uq_r[:, :, :half]], axis=2)
    zpad = jnp.zeros((Q_LORA, MLA_HEADS, LANES - MLA_ROPE), w_uq.dtype)
    uq_r = jnp.concatenate([uq_r, zpad], axis=2).reshape(Q_LORA, MLA_HEADS * LANES)
    uq_rot = jnp.concatenate([uq_rot, zpad], axis=2).reshape(Q_LORA, MLA_HEADS * LANES)
    wuq_all = jnp.concatenate([uq_nope, uq_r, uq_rot], axis=1)
    return w_all.astype(BF16), wg_t.astype(BF16), gbias.astype(F32), wuq_all.astype(BF16)


def kernel(x, positions, w_in, q_norm_g, kv_norm_g, w_uq, w_uk, w_uv, b_igate, b_fgate,
           ml_head_g, beta_mla, beta_ml, w_out, ln1_g, ln1_b, w_ffn_gate, w_ffn_val,
           conv_w, conv_b, w_down, ln2_g, ln2_b):
    B, S, D = x.shape
    T = B * S
    assert D == D_MODEL and w_in.shape[0] == DEPTH == 1
    assert S % (2 * ATT_T) == 0 and S % ML_L == 0 and S % FFN_TM == 0 and T % PROJ_TM == 0
    l = 0
    x2 = x.reshape(T, D)
    pos_col = positions.reshape(T, 1).astype(F32)
    inv_freq = 1.0 / (ROPE_BASE ** (jnp.arange(0, MLA_ROPE, 2, dtype=F32) / MLA_ROPE))
    invf = jnp.tile(inv_freq, LANES // (MLA_ROPE // 2))[None, :]

    w_all, wg_t, gbias, wuq_all = _prep_proj_weights(w_in[l], w_uq[l], b_igate[l], b_fgate[l])
    q_pad, k_pad, vt, mq, mk, mv, mo, gates_t = _proj_call(
        x2, pos_col, invf, w_all, wg_t, gbias,
        q_norm_g[l][None, :], kv_norm_g[l][None, :], wuq_all,
        w_uk[l].astype(BF16), w_uv[l].T.astype(BF16))

    attn = _flash_call(q_pad, k_pad, vt, beta_mla[l][None, :], B, S)
    mem = _mlstm_call(mq, mk, mv, mo, gates_t, ml_head_g[l][None, :], beta_ml[l][None, :], B, S)
    h1 = _outproj_call(attn, mem, x2, w_out[l].astype(BF16), ln1_g[l][None, :], ln1_b[l][None, :])
    out = _ffn_call(h1, w_ffn_gate[l].astype(BF16), w_ffn_val[l].astype(BF16),
                    w_down[l].astype(BF16), conv_w[l], conv_b[l][None, :],
                    ln2_g[l][None, :], ln2_b[l][None, :], S)
    return out.reshape(B, S, D)
```

```python
import functools
import math

import jax
import jax.numpy as jnp
from jax import lax
from jax.experimental import pallas as pl
from jax.experimental.pallas import tpu as pltpu

D_MODEL = 2048
CHUNK = 64
MLA_HEADS = 8
MLA_NOPE = 128
MLA_ROPE = 64
MLA_V = 128
Q_LORA = 512
KV_LORA = 256
ROPE_BASE = 10000.0
MLA_OUT = MLA_HEADS * MLA_V
ML_HEADS = 4
ML_QK = 128
ML_V = 256
ML_OUT = ML_HEADS * ML_V
GATE_CAP = 15.0
D_FF = 5632
RMS_EPS = 1e-6
LN_EPS = 1e-5
DEPTH = 1
DN_ALPHA = (2 * DEPTH) ** 0.25

LANES = 128
V7X_VMEM_LIMIT_BYTES = 58 * 1024 * 1024

PROJ_TM = 256
ATT_T = 512
ML_L = 256
OUT_TM = 512
OUT_SUBTILES = 4
FFN_TM = 512
FFN_TF = 512
FFN_SUB = 256

HEAD_PAD = 2 * LANES
PROJ_W = 4096
_C_CQ, _C_CKV, _C_KR, _C_KROT, _C_MQ, _C_MK, _C_MV, _C_MO = 0, 512, 768, 896, 1024, 1536, 2048, 3072

F32 = jnp.float32
BF16 = jnp.bfloat16
NEG_BIG = -1e30


def _nt_dot(a, b):
    return lax.dot_general(a, b, (((1,), (1,)), ((), ())), preferred_element_type=F32)


def _dot(a, b):
    return jnp.dot(a, b, preferred_element_type=F32)


def _const_spec(shape):
    nd = len(shape)
    return pl.BlockSpec(shape, lambda *_: (0,) * nd, pipeline_mode=pl.Buffered(1))


def _proj_kernel(x_ref, pos_ref, invf_ref, w_ref, wg_ref, gb_ref, qg_ref, kvg_ref,
                 wuq_ref, wuk_ref, wuv_ref,
                 q_ref, k_ref, vt_ref, mq_ref, mk_ref, mv_ref, mo_ref, gt_ref,
                 xb_ref):
    xb_ref[...] = x_ref[...].astype(BF16)
    xb = xb_ref[...]
    tm = xb.shape[0]

    cq = _dot(xb, w_ref[:, _C_CQ:_C_CQ + Q_LORA])
    p1 = _dot(xb, w_ref[:, _C_CKV:_C_MQ])
    mq_ref[...] = _dot(xb, w_ref[:, _C_MQ:_C_MK]).astype(BF16)
    mk_ref[...] = (_dot(xb, w_ref[:, _C_MK:_C_MV]) * (ML_QK ** -0.5)).astype(BF16)

    ang = pos_ref[...] * invf_ref[...]
    cos = jnp.cos(ang)
    sin = jnp.sin(ang)

    cq = cq * lax.rsqrt(jnp.mean(cq * cq, axis=-1, keepdims=True) + RMS_EPS) * qg_ref[...]
    cqb = cq.astype(BF16)
    ckv = p1[:, 0:KV_LORA]
    ckv = ckv * lax.rsqrt(jnp.mean(ckv * ckv, axis=-1, keepdims=True) + RMS_EPS) * kvg_ref[...]
    ckvb = ckv.astype(BF16)
    k_rope = (p1[:, 256:384] * cos + p1[:, 384:512] * sin).astype(BF16)

    qscale = (MLA_NOPE + MLA_ROPE) ** -0.5 * math.log2(math.e)
    q_nope = _dot(cqb, wuq_ref[:, 0:1024]) * qscale
    q_r = _dot(cqb, wuq_ref[:, 1024:2048])
    q_rot = _dot(cqb, wuq_ref[:, 2048:3072])
    k_nope = _dot(ckvb, wuk_ref[...]).astype(BF16)
    vt_ref[0] = _nt_dot(wuv_ref[...], ckvb).astype(BF16)
    mv_ref[...] = _dot(xb, w_ref[:, _C_MV:_C_MO]).astype(BF16)
    mo_ref[...] = _dot(xb, w_ref[:, _C_MO:PROJ_W]).astype(BF16)
    gt_ref[...] = _nt_dot(wg_ref[...], xb) + gb_ref[...]

    cos8 = jnp.concatenate([cos] * MLA_HEADS, axis=1)
    sin8 = jnp.concatenate([sin] * MLA_HEADS, axis=1)
    q_rope = (q_r * cos8 + q_rot * sin8) * qscale
    for h in range(MLA_HEADS):
        q_ref[:, h * HEAD_PAD:h * HEAD_PAD + LANES] = q_nope[:, h * LANES:(h + 1) * LANES].astype(BF16)
        q_ref[:, h * HEAD_PAD + LANES:(h + 1) * HEAD_PAD] = q_rope[:, h * LANES:(h + 1) * LANES].astype(BF16)
        k_ref[:, h * HEAD_PAD:h * HEAD_PAD + LANES] = k_nope[:, h * LANES:(h + 1) * LANES]
        k_ref[:, h * HEAD_PAD + LANES:(h + 1) * HEAD_PAD] = k_rope


def _proj_call(x2, pos_col, invf, w_all, wg_t, gbias, qg, kvg, wuq_all, wuk, wuv):
    T = x2.shape[0]
    tm = PROJ_TM
    row = lambda w: pl.BlockSpec((tm, w), lambda i: (i, 0))
    out_shapes = (
        jax.ShapeDtypeStruct((T, MLA_HEADS * HEAD_PAD), BF16),
        jax.ShapeDtypeStruct((T, MLA_HEADS * HEAD_PAD), BF16),
        jax.ShapeDtypeStruct((T // ATT_T, MLA_OUT, ATT_T), BF16),
        jax.ShapeDtypeStruct((T, ML_HEADS * ML_QK), BF16),
        jax.ShapeDtypeStruct((T, ML_HEADS * ML_QK), BF16),
        jax.ShapeDtypeStruct((T, ML_OUT), BF16),
        jax.ShapeDtypeStruct((T, ML_OUT), BF16),
        jax.ShapeDtypeStruct((8, T), F32),
    )
    return pl.pallas_call(
        _proj_kernel,
        grid=(T // tm,),
        in_specs=[
            row(D_MODEL),
            pl.BlockSpec((tm, 1), lambda i: (i, 0)),
            _const_spec((1, LANES)),
            _const_spec((D_MODEL, PROJ_W)),
            _const_spec((8, D_MODEL)),
            _const_spec((8, 1)),
            _const_spec((1, Q_LORA)),
            _const_spec((1, KV_LORA)),
            _const_spec((Q_LORA, 3072)),
            _const_spec((KV_LORA, MLA_HEADS * MLA_NOPE)),
            _const_spec((MLA_OUT, KV_LORA)),
        ],
        out_specs=(
            row(MLA_HEADS * HEAD_PAD), row(MLA_HEADS * HEAD_PAD),
            pl.BlockSpec((1, MLA_OUT, tm), lambda i: (i // (ATT_T // tm), 0, i % (ATT_T // tm))),
            row(ML_HEADS * ML_QK), row(ML_HEADS * ML_QK), row(ML_OUT), row(ML_OUT),
            pl.BlockSpec((8, tm), lambda i: (0, i)),
        ),
        out_shape=out_shapes,
        scratch_shapes=[pltpu.VMEM((tm, D_MODEL), BF16)],
        compiler_params=pltpu.CompilerParams(
            dimension_semantics=("arbitrary",), vmem_limit_bytes=V7X_VMEM_LIMIT_BYTES),
        name="proj_mla_prep",
    )(x2, pos_col, invf, w_all, wg_t, gbias, qg, kvg, wuq_all, wuk, wuv)


def _scores(q, k):
    return _nt_dot(k, q)


def _chunk_causal(s):
    t = s.shape[0]
    kc = lax.broadcasted_iota(jnp.int32, (t, t), 0) // CHUNK
    qc = lax.broadcasted_iota(jnp.int32, (t, t), 1) // CHUNK
    return jnp.where(kc <= qc, s, -jnp.inf)


def _softmax_update(state, s, vt):
    m_prev, l_prev, acc = state
    m_new = jnp.maximum(m_prev, jnp.max(s, axis=0, keepdims=True))
    alpha = jnp.exp2(m_prev - m_new)
    p = jnp.exp2(s - m_new)
    l_new = alpha * l_prev + jnp.sum(p, axis=0, keepdims=True)
    acc_new = alpha * acc + _dot(vt, p.astype(BF16))
    return m_new, l_new, acc_new


def _flash_kernel(q_ref, k_ref, vt_ref, beta_ref, qn_ref, kf_ref, o_ref, m_sc, l_sc, acc_sc, s_sc):
    i = pl.program_id(2)
    t = ATT_T
    m_sc[...] = jnp.full(m_sc.shape, NEG_BIG, F32)
    l_sc[...] = jnp.zeros(l_sc.shape, F32)
    acc_sc[...] = jnp.zeros(acc_sc.shape, F32)

    def keys(j):
        return k_ref[pl.ds(pl.multiple_of(j * t, t), t), :]

    def load(half):
        return m_sc[half], l_sc[half], acc_sc[half]

    def store(half, state):
        m_sc[half], l_sc[half], acc_sc[half] = state

    q0 = q_ref[0:t, :]
    q1 = q_ref[t:2 * t, :]

    first_step = (pl.program_id(0) == 0) & (pl.program_id(1) == 0) & (i == 0)

    @pl.when(first_step)
    def _():
        kn0, kn1 = keys(0), keys(1)
        s_sc[0] = _scores(q0, kn0)
        s_sc[1] = _scores(q1, kn0)
        s_sc[2] = _scores(q0, kn1)
        s_sc[3] = _scores(q1, kn1)

    def body(jj, carry):
        v0, v1 = vt_ref[2 * jj], vt_ref[2 * jj + 1]
        kn0, kn1 = keys(2 * jj + 2), keys(2 * jj + 3)
        sn0 = _scores(q0, kn0)
        sn1 = _scores(q1, kn0)
        st0 = _softmax_update(load(0), s_sc[0], v0)
        s_sc[0] = sn0
        st1 = _softmax_update(load(1), s_sc[1], v0)
        s_sc[1] = sn1
        sn2 = _scores(q0, kn1)
        store(0, _softmax_update(st0, s_sc[2], v1))
        s_sc[2] = sn2
        store(1, _softmax_update(st1, s_sc[3], v1))
        s_sc[3] = _scores(q1, kn1)
        return carry

    lax.fori_loop(0, i, body, 0)
    v0, v1 = vt_ref[2 * i], vt_ref[2 * i + 1]
    qn0, qn1 = qn_ref[0:t, :], qn_ref[t:2 * t, :]
    kf0, kf1 = kf_ref[0:t, :], kf_ref[t:2 * t, :]
    sn0 = _scores(qn0, kf0)
    sn1 = _scores(qn1, kf0)
    st0 = _softmax_update(load(0), _chunk_causal(s_sc[0]), v0)
    s_sc[0] = sn0
    st1 = _softmax_update(load(1), s_sc[1], v0)
    s_sc[1] = sn1
    st1 = _softmax_update(st1, _chunk_causal(s_sc[3]), v1)
    sn2 = _scores(qn0, kf1)
    sn3 = _scores(qn1, kf1)
    for half, (_, l_fin, acc_fin) in enumerate((st0, st1)):
        out = jnp.transpose(acc_fin / l_fin) * beta_ref[...]
        o_ref[half * t:(half + 1) * t, :] = out.astype(BF16)
    s_sc[2] = sn2
    s_sc[3] = sn3


def _flash_call(q_pad, k_pad, vt, beta_mla, batch, seq):
    T = q_pad.shape[0]
    t = ATT_T
    tq = 2 * t
    nq = seq // tq
    nkv = seq // t
    n_steps = batch * MLA_HEADS * nq

    def successor(b, h, i):
        g = jnp.minimum((b * MLA_HEADS + h) * nq + i + 1, n_steps - 1)
        return g // (nq * MLA_HEADS), (g // nq) % MLA_HEADS, g % nq

    def next_q(b, h, i):
        b2, h2, i2 = successor(b, h, i)
        return b2 * nq + i2, h2

    def next_first_keys(b, h, i):
        b2, h2, _ = successor(b, h, i)
        return b2 * (seq // tq), h2

    return pl.pallas_call(
        _flash_kernel,
        grid=(batch, MLA_HEADS, nq),
        in_specs=[
            pl.BlockSpec((tq, HEAD_PAD), lambda b, h, i: (b * nq + i, h)),
            pl.BlockSpec((seq, HEAD_PAD), lambda b, h, i: (b, h)),
            pl.BlockSpec((nkv, MLA_V, t), lambda b, h, i: (b, h, 0)),
            pl.BlockSpec((1, MLA_V), lambda b, h, i: (0, h)),
            pl.BlockSpec((tq, HEAD_PAD), next_q),
            pl.BlockSpec((tq, HEAD_PAD), next_first_keys),
        ],
        out_specs=pl.BlockSpec((tq, MLA_V), lambda b, h, i: (b * nq + i, h)),
        out_shape=jax.ShapeDtypeStruct((T, MLA_OUT), BF16),
        scratch_shapes=[pltpu.VMEM((2, 1, t), F32), pltpu.VMEM((2, 1, t), F32),
                        pltpu.VMEM((2, MLA_V, t), F32), pltpu.VMEM((4, t, t), F32)],
        compiler_params=pltpu.CompilerParams(
            dimension_semantics=("arbitrary", "arbitrary", "arbitrary"),
            vmem_limit_bytes=V7X_VMEM_LIMIT_BYTES),
        name="mla_flash",
    )(q_pad, k_pad, vt, beta_mla, q_pad, k_pad)


def _lane_cumsum(x):
    n = x.shape[-1]
    lane = lax.broadcasted_iota(jnp.int32, x.shape, x.ndim - 1)
    d = 1
    while d < n:
        x = x + jnp.where(lane >= d, pltpu.roll(x, d, x.ndim - 1), 0.0)
        d *= 2
    return x


def _mlstm_kernel(q_ref, k_ref, v_ref, o_ref, gt_ref, hg_ref, beta_ref, out_ref,
                  c_sc, n_sc, m_sc):
    L = ML_L

    @pl.when(pl.program_id(1) == 0)
    def _():
        c_sc[...] = jnp.zeros(c_sc.shape, F32)
        n_sc[...] = jnp.zeros(n_sc.shape, F32)
        m_sc[...] = jnp.full(m_sc.shape, -jnp.inf, F32)

    g8 = gt_ref[...]
    capped = GATE_CAP * jnp.tanh(g8 / GATE_CAP)
    logf = jnp.minimum(capped, 0.0) - jnp.log1p(jnp.exp(-jnp.abs(capped)))
    row8 = lax.broadcasted_iota(jnp.int32, g8.shape, 0)
    bsum = _lane_cumsum(jnp.where(row8 >= ML_HEADS, logf, 0.0))
    gb8 = jnp.where(row8 < ML_HEADS, capped - pltpu.roll(bsum, ML_HEADS, 0), bsum)

    rr = lax.broadcasted_iota(jnp.int32, (L, L), 0)
    cc = lax.broadcasted_iota(jnp.int32, (L, L), 1)
    causal = cc <= rr
    eye = cc == rr

    for h in range(ML_HEADS):
        g_row = gb8[h:h + 1, :]
        b_row = gb8[ML_HEADS + h:ML_HEADS + h + 1, :]
        G = jnp.broadcast_to(g_row, (L, L))
        Bm = jnp.broadcast_to(b_row, (L, L))
        m_prev = m_sc[h:h + 1, 0:1]
        cummax = jnp.max(jnp.where(causal, G, -jnp.inf), axis=1, keepdims=True)
        mcol = jnp.maximum(m_prev, cummax)
        g_col = jnp.sum(jnp.where(eye, G, 0.0), axis=1, keepdims=True)
        b_col = jnp.sum(jnp.where(eye, Bm, 0.0), axis=1, keepdims=True)
        dmat = jnp.where(causal, jnp.exp(G - mcol), 0.0)
        inter_w = jnp.exp(m_prev - mcol)

        qh = q_ref[:, h * ML_QK:(h + 1) * ML_QK]
        kh = k_ref[:, h * ML_QK:(h + 1) * ML_QK]
        vh = v_ref[:, h * ML_V:(h + 1) * ML_V]
        ct = c_sc[h]
        nrow = n_sc[h:h + 1, :]

        s = _nt_dot(qh, kh) * dmat
        num = _dot(s.astype(BF16), vh) + inter_w * _dot(qh, ct.astype(BF16))
        den = (jnp.sum(s, axis=1, keepdims=True)
               + inter_w * jnp.sum(qh.astype(F32) * nrow, axis=1, keepdims=True))
        hval = num / jnp.maximum(jnp.abs(den), jnp.exp(-(b_col + mcol)))

        m_last = mcol[L - 1:L, :]
        decay = inter_w[L - 1:L, :]
        w_col = jnp.exp(g_col - m_last)
        kw = kh.astype(F32) * w_col
        upd = lax.dot_general(kw.astype(BF16), vh, (((0,), (0,)), ((), ())),
                              preferred_element_type=F32)
        c_sc[h] = decay * ct + upd
        n_sc[h:h + 1, :] = decay * nrow + jnp.sum(kw, axis=0, keepdims=True)
        m_sc[h:h + 1, :] = jnp.broadcast_to(b_row[:, L - 1:L] + m_last, (1, LANES))

        hn = hval * lax.rsqrt(jnp.mean(hval * hval, axis=-1, keepdims=True) + RMS_EPS)
        hn = hn * hg_ref[:, h * ML_V:(h + 1) * ML_V]
        og = jax.nn.sigmoid(o_ref[:, h * ML_V:(h + 1) * ML_V].astype(F32))
        out_ref[:, h * ML_V:(h + 1) * ML_V] = (
            og * hn * beta_ref[:, h * ML_V:(h + 1) * ML_V]).astype(BF16)


def _mlstm_call(mq, mk, mv, mo, gates_t, head_g, beta_ml, batch, seq):
    T = mq.shape[0]
    L = ML_L
    nc = seq // L
    blk = lambda w: pl.BlockSpec((L, w), lambda b, c: (b * nc + c, 0))
    return pl.pallas_call(
        _mlstm_kernel,
        grid=(batch, nc),
        in_specs=[
            blk(ML_HEADS * ML_QK), blk(ML_HEADS * ML_QK), blk(ML_OUT), blk(ML_OUT),
            pl.BlockSpec((8, L), lambda b, c: (0, b * nc + c)),
            _const_spec((1, ML_OUT)),
            _const_spec((1, ML_OUT)),
        ],
        out_specs=blk(ML_OUT),
        out_shape=jax.ShapeDtypeStruct((T, ML_OUT), BF16),
        scratch_shapes=[pltpu.VMEM((ML_HEADS, ML_QK, ML_V), F32),
                        pltpu.VMEM((8, ML_QK), F32),
                        pltpu.VMEM((8, LANES), F32)],
        compiler_params=pltpu.CompilerParams(
            dimension_semantics=("arbitrary", "arbitrary"),
            vmem_limit_bytes=V7X_VMEM_LIMIT_BYTES),
        name="mlstm_scan",
    )(mq, mk, mv, mo, gates_t, head_g, beta_ml)


def _layernorm(y, g, b):
    mu = jnp.mean(y, axis=-1, keepdims=True)
    yc = y - mu
    return yc * lax.rsqrt(jnp.mean(yc * yc, axis=-1, keepdims=True) + LN_EPS) * g + b


def _outproj_kernel(a_ref, m_ref, x_ref, w_ref, g_ref, b_ref, o_ref):
    tm = o_ref.shape[0]
    sub = tm // OUT_SUBTILES

    def rows(r):
        return slice(r * sub, (r + 1) * sub)

    def project(r):
        return _dot(a_ref[rows(r), :], w_ref[0:MLA_OUT, :]) + _dot(m_ref[rows(r), :], w_ref[MLA_OUT:, :])

    def finish(r, acc):
        o_ref[rows(r), :] = _layernorm(DN_ALPHA * x_ref[rows(r), :] + acc, g_ref[...], b_ref[...])

    acc = project(0)
    for r in range(1, OUT_SUBTILES):
        nxt = project(r)
        finish(r - 1, acc)
        acc = nxt
    finish(OUT_SUBTILES - 1, acc)


def _outproj_call(attn, mem, x2, w_out, g, b):
    T = x2.shape[0]
    tm = OUT_TM
    return pl.pallas_call(
        _outproj_kernel,
        grid=(T // tm,),
        in_specs=[
            pl.BlockSpec((tm, MLA_OUT), lambda i: (i, 0)),
            pl.BlockSpec((tm, ML_OUT), lambda i: (i, 0)),
            pl.BlockSpec((tm, D_MODEL), lambda i: (i, 0)),
            _const_spec((MLA_OUT + ML_OUT, D_MODEL)),
            _const_spec((1, D_MODEL)),
            _const_spec((1, D_MODEL)),
        ],
        out_specs=pl.BlockSpec((tm, D_MODEL), lambda i: (i, 0)),
        out_shape=jax.ShapeDtypeStruct((T, D_MODEL), F32),
        compiler_params=pltpu.CompilerParams(
            dimension_semantics=("arbitrary",), vmem_limit_bytes=V7X_VMEM_LIMIT_BYTES),
        name="outproj_ln",
    )(attn, mem, x2, w_out, g, b)


def _gelu_tanh(x):
    c = math.sqrt(2.0 / math.pi)
    return 0.5 * x * (1.0 + jnp.tanh(c * (x + 0.044715 * (x * x * x))))


def _ffn_kernel(tiles_per_seq, h_ref, wg_ref, wv_ref, wd_ref, cw_ref, cb_ref, g_ref, b_ref,
                o_ref, hb_ref, gbuf_ref, val_ref, halo_ref):
    i = pl.program_id(0)
    f = pl.program_id(1)
    nf = pl.num_programs(1) - 1
    tm = FFN_TM
    cols = [slice(c * FFN_SUB, (c + 1) * FFN_SUB) for c in range(FFN_TF // FFN_SUB)]
    first = (i % tiles_per_seq) == 0

    def up_project(c):
        hb = hb_ref[...]
        return _dot(hb, wg_ref[:, c]), _dot(hb, wv_ref[:, c])

    def stash(c, gate, val):
        gbuf_ref[0:8, c] = jnp.where(first, 0.0, halo_ref[f, :, c])
        gbuf_ref[8:8 + tm, c] = gate
        halo_ref[f, :, c] = gate[tm - 8:tm, :]
        val_ref[:, c] = val

    def down_project(c):
        conv = (cw_ref[0:1, c] * gbuf_ref[6:6 + tm, c] + cw_ref[1:2, c] * gbuf_ref[7:7 + tm, c]
                + cw_ref[2:3, c] * gbuf_ref[8:8 + tm, c] + cb_ref[:, c])
        act = (_gelu_tanh(conv) * val_ref[:, c]).astype(BF16)
        return _dot(act, wd_ref[c, :])

    def consume():
        acc = None
        for c in cols:
            part = down_project(c)
            acc = part if acc is None else acc + part
        o_ref[...] += acc

    @pl.when(jnp.logical_and(i == 0, f == 0))
    def _():
        halo_ref[...] = jnp.zeros(halo_ref.shape, F32)

    @pl.when(f == 0)
    def _():
        hb_ref[...] = h_ref[...].astype(BF16)
        o_ref[...] = DN_ALPHA * h_ref[...]
        for c in cols:
            stash(c, *up_project(c))

    @pl.when(jnp.logical_and(f > 0, f < nf))
    def _():
        ups = [up_project(c) for c in cols]
        consume()
        for c, (gate, val) in zip(cols, ups):
            stash(c, gate, val)

    @pl.when(f == nf)
    def _():
        consume()
        o_ref[...] = _layernorm(o_ref[...], g_ref[...], b_ref[...])


def _ffn_call(h1, wg, wv, wd, conv_w, conv_b, g, b, seq):
    T = h1.shape[0]
    tm, tf = FFN_TM, FFN_TF
    nf = D_FF // tf
    kern = functools.partial(_ffn_kernel, seq // tm)
    produced = lambda i, f: (0, jnp.minimum(f, nf - 1))
    consumed = lambda i, f: (0, jnp.maximum(f - 1, 0))
    return pl.pallas_call(
        kern,
        grid=(T // tm, nf + 1),
        in_specs=[
            pl.BlockSpec((tm, D_MODEL), lambda i, f: (i, 0)),
            pl.BlockSpec((D_MODEL, tf), produced),
            pl.BlockSpec((D_MODEL, tf), produced),
            pl.BlockSpec((tf, D_MODEL), lambda i, f: (jnp.maximum(f - 1, 0), 0)),
            pl.BlockSpec((3, tf), consumed),
            pl.BlockSpec((1, tf), consumed),
            _const_spec((1, D_MODEL)),
            _const_spec((1, D_MODEL)),
        ],
        out_specs=pl.BlockSpec((tm, D_MODEL), lambda i, f: (i, 0)),
        out_shape=jax.ShapeDtypeStruct((T, D_MODEL), F32),
        scratch_shapes=[pltpu.VMEM((tm, D_MODEL), BF16),
                        pltpu.VMEM((tm + 8, tf), F32),
                        pltpu.VMEM((tm, tf), F32),
                        pltpu.VMEM((nf, 8, tf), F32)],
        compiler_params=pltpu.CompilerParams(
            dimension_semantics=("arbitrary", "arbitrary"),
            vmem_limit_bytes=V7X_VMEM_LIMIT_BYTES),
        name="convglu_ffn_ln",
    )(h1, wg, wv, wd, conv_w, conv_b, g, b)


def _prep_proj_weights(w_in, w_uq, b_igate, b_fgate):
    half = MLA_ROPE // 2
    o = 0
    w_cq = w_in[:, o:o + Q_LORA]; o += Q_LORA
    w_ckv = w_in[:, o:o + KV_LORA]; o += KV_LORA
    w_kr = w_in[:, o:o + MLA_ROPE]; o += MLA_ROPE
    w_mq = w_in[:, o:o + ML_HEADS * ML_QK]; o += ML_HEADS * ML_QK
    w_mk = w_in[:, o:o + ML_HEADS * ML_QK]; o += ML_HEADS * ML_QK
    w_mv = w_in[:, o:o + ML_OUT]; o += ML_OUT
    w_mo = w_in[:, o:o + ML_OUT]; o += ML_OUT
    w_gates = w_in[:, o:o + 2 * ML_HEADS]
    w_krot = jnp.concatenate([-w_kr[:, half:], w_kr[:, :half]], axis=1)
    z64 = jnp.zeros((D_MODEL, LANES - MLA_ROPE), w_in.dtype)
    w_all = jnp.concatenate([w_cq, w_ckv, w_kr, z64, w_krot, z64, w_mq, w_mk, w_mv, w_mo], axis=1)
    wg_t = w_gates.T
    gbias = jnp.concatenate([b_igate, b_fgate])[:, None]

    uq = w_uq.reshape(Q_LORA, MLA_HEADS, MLA_NOPE + MLA_ROPE)
    uq_nope = uq[:, :, :MLA_NOPE].reshape(Q_LORA, MLA_HEADS * MLA_NOPE)
    uq_r = uq[:, :, MLA_NOPE:]
    uq_rot = jnp.concatenate([-uq_r[:, :, half:], uq_r[:, :, :half]], axis=2)
    zpad = jnp.zeros((Q_LORA, MLA_HEADS, LANES - MLA_ROPE), w_uq.dtype)
    uq_r = jnp.concatenate([uq_r, zpad], axis=2).reshape(Q_LORA, MLA_HEADS * LANES)
    uq_rot = jnp.concatenate([uq_rot, zpad], axis=2).reshape(Q_LORA, MLA_HEADS * LANES)
    wuq_all = jnp.concatenate([uq_nope, uq_r, uq_rot], axis=1)
    return w_all.astype(BF16), wg_t.astype(BF16), gbias.astype(F32), wuq_all.astype(BF16)


def kernel(x, positions, w_in, q_norm_g, kv_norm_g, w_uq, w_uk, w_uv, b_igate, b_fgate,
           ml_head_g, beta_mla, beta_ml, w_out, ln1_g, ln1_b, w_ffn_gate, w_ffn_val,
           conv_w, conv_b, w_down, ln2_g, ln2_b):
    B, S, D = x.shape
    T = B * S
    assert D == D_MODEL and w_in.shape[0] == DEPTH == 1
    assert S % (2 * ATT_T) == 0 and S % ML_L == 0 and S % FFN_TM == 0 and T % PROJ_TM == 0
    l = 0
    x2 = x.reshape(T, D)
    pos_col = positions.reshape(T, 1).astype(F32)
    inv_freq = 1.0 / (ROPE_BASE ** (jnp.arange(0, MLA_ROPE, 2, dtype=F32) / MLA_ROPE))
    invf = jnp.tile(inv_freq, LANES // (MLA_ROPE // 2))[None, :]

    w_all, wg_t, gbias, wuq_all = _prep_proj_weights(w_in[l], w_uq[l], b_igate[l], b_fgate[l])
    q_pad, k_pad, vt, mq, mk, mv, mo, gates_t = _proj_call(
        x2, pos_col, invf, w_all, wg_t, gbias,
        q_norm_g[l][None, :], kv_norm_g[l][None, :], wuq_all,
        w_uk[l].astype(BF16), w_uv[l].T.astype(BF16))

    attn = _flash_call(q_pad, k_pad, vt, beta_mla[l][None, :], B, S)
    mem = _mlstm_call(mq, mk, mv, mo, gates_t, ml_head_g[l][None, :], beta_ml[l][None, :], B, S)
    h1 = _outproj_call(attn, mem, x2, w_out[l].astype(BF16), ln1_g[l][None, :], ln1_b[l][None, :])
    out = _ffn_call(h1, w_ffn_gate[l].astype(BF16), w_ffn_val[l].astype(BF16),
                    w_down[l].astype(BF16), conv_w[l], conv_b[l][None, :],
                    ln2_g[l][None, :], ln2_b[l][None, :], S)
    return out.reshape(B, S, D)
```

```python
import functools
import math

import jax
import jax.numpy as jnp
from jax import lax
from jax.experimental import pallas as pl
from jax.experimental.pallas import tpu as pltpu

D_MODEL = 2048
CHUNK = 64
MLA_HEADS = 8
MLA_NOPE = 128
MLA_ROPE = 64
MLA_V = 128
Q_LORA = 512
KV_LORA = 256
ROPE_BASE = 10000.0
MLA_OUT = MLA_HEADS * MLA_V
ML_HEADS = 4
ML_QK = 128
ML_V = 256
ML_OUT = ML_HEADS * ML_V
GATE_CAP = 15.0
D_FF = 5632
RMS_EPS = 1e-6
LN_EPS = 1e-5
DEPTH = 1
DN_ALPHA = (2 * DEPTH) ** 0.25

LANES = 128
V7X_VMEM_LIMIT_BYTES = 58 * 1024 * 1024

PROJ_TM = 256
ATT_T = 512
ML_L = 256
OUT_TM = 512
OUT_SUBTILES = 4
FFN_TM = 512
FFN_TF = 512
FFN_SUB = 256

HEAD_PAD = 2 * LANES
PROJ_W = 4096
_C_CQ, _C_CKV, _C_KR, _C_KROT, _C_MQ, _C_MK, _C_MV, _C_MO = 0, 512, 768, 896, 1024, 1536, 2048, 3072

F32 = jnp.float32
BF16 = jnp.bfloat16
NEG_BIG = -1e30


def _nt_dot(a, b):
    return lax.dot_general(a, b, (((1,), (1,)), ((), ())), preferred_element_type=F32)


def _dot(a, b):
    return jnp.dot(a, b, preferred_element_type=F32)


def _const_spec(shape):
    nd = len(shape)
    return pl.BlockSpec(shape, lambda *_: (0,) * nd, pipeline_mode=pl.Buffered(1))


def _proj_kernel(x_ref, pos_ref, invf_ref, w_ref, wg_ref, gb_ref, qg_ref, kvg_ref,
                 wuq_ref, wuk_ref, wuv_ref,
                 q_ref, k_ref, vt_ref, mq_ref, mk_ref, mv_ref, mo_ref, gt_ref,
                 xb_ref):
    xb_ref[...] = x_ref[...].astype(BF16)
    xb = xb_ref[...]
    tm = xb.shape[0]

    cq = _dot(xb, w_ref[:, _C_CQ:_C_CQ + Q_LORA])
    p1 = _dot(xb, w_ref[:, _C_CKV:_C_MQ])
    mq_ref[...] = _dot(xb, w_ref[:, _C_MQ:_C_MK]).astype(BF16)
    mk_ref[...] = (_dot(xb, w_ref[:, _C_MK:_C_MV]) * (ML_QK ** -0.5)).astype(BF16)

    ang = pos_ref[...] * invf_ref[...]
    cos = jnp.cos(ang)
    sin = jnp.sin(ang)

    cq = cq * lax.rsqrt(jnp.mean(cq * cq, axis=-1, keepdims=True) + RMS_EPS) * qg_ref[...]
    cqb = cq.astype(BF16)
    ckv = p1[:, 0:KV_LORA]
    ckv = ckv * lax.rsqrt(jnp.mean(ckv * ckv, axis=-1, keepdims=True) + RMS_EPS) * kvg_ref[...]
    ckvb = ckv.astype(BF16)
    k_rope = (p1[:, 256:384] * cos + p1[:, 384:512] * sin).astype(BF16)

    qscale = (MLA_NOPE + MLA_ROPE) ** -0.5 * math.log2(math.e)
    q_nope = _dot(cqb, wuq_ref[:, 0:1024]) * qscale
    q_r = _dot(cqb, wuq_ref[:, 1024:2048])
    q_rot = _dot(cqb, wuq_ref[:, 2048:3072])
    k_nope = _dot(ckvb, wuk_ref[...]).astype(BF16)
    vt_ref[0] = _nt_dot(wuv_ref[...], ckvb).astype(BF16)
    mv_ref[...] = _dot(xb, w_ref[:, _C_MV:_C_MO]).astype(BF16)
    mo_ref[...] = _dot(xb, w_ref[:, _C_MO:PROJ_W]).astype(BF16)
    gt_ref[...] = _nt_dot(wg_ref[...], xb) + gb_ref[...]

    cos8 = jnp.concatenate([cos] * MLA_HEADS, axis=1)
    sin8 = jnp.concatenate([sin] * MLA_HEADS, axis=1)
    q_rope = (q_r * cos8 + q_rot * sin8) * qscale
    for h in range(MLA_HEADS):
        q_ref[:, h * HEAD_PAD:h * HEAD_PAD + LANES] = q_nope[:, h * LANES:(h + 1) * LANES].astype(BF16)
        q_ref[:, h * HEAD_PAD + LANES:(h + 1) * HEAD_PAD] = q_rope[:, h * LANES:(h + 1) * LANES].astype(BF16)
        k_ref[:, h * HEAD_PAD:h * HEAD_PAD + LANES] = k_nope[:, h * LANES:(h + 1) * LANES]
        k_ref[:, h * HEAD_PAD + LANES:(h + 1) * HEAD_PAD] = k_rope


def _proj_call(x2, pos_col, invf, w_all, wg_t, gbias, qg, kvg, wuq_all, wuk, wuv):
    T = x2.shape[0]
    tm = PROJ_TM
    row = lambda w: pl.BlockSpec((tm, w), lambda i: (i, 0))
    out_shapes = (
        jax.ShapeDtypeStruct((T, MLA_HEADS * HEAD_PAD), BF16),
        jax.ShapeDtypeStruct((T, MLA_HEADS * HEAD_PAD), BF16),
        jax.ShapeDtypeStruct((T // ATT_T, MLA_OUT, ATT_T), BF16),
        jax.ShapeDtypeStruct((T, ML_HEADS * ML_QK), BF16),
        jax.ShapeDtypeStruct((T, ML_HEADS * ML_QK), BF16),
        jax.ShapeDtypeStruct((T, ML_OUT), BF16),
        jax.ShapeDtypeStruct((T, ML_OUT), BF16),
        jax.ShapeDtypeStruct((8, T), F32),
    )
    return pl.pallas_call(
        _proj_kernel,
        grid=(T // tm,),
        in_specs=[
            row(D_MODEL),
            pl.BlockSpec((tm, 1), lambda i: (i, 0)),
            _const_spec((1, LANES)),
            _const_spec((D_MODEL, PROJ_W)),
            _const_spec((8, D_MODEL)),
            _const_spec((8, 1)),
            _const_spec((1, Q_LORA)),
            _const_spec((1, KV_LORA)),
            _const_spec((Q_LORA, 3072)),
            _const_spec((KV_LORA, MLA_HEADS * MLA_NOPE)),
            _const_spec((MLA_OUT, KV_LORA)),
        ],
        out_specs=(
            row(MLA_HEADS * HEAD_PAD), row(MLA_HEADS * HEAD_PAD),
            pl.BlockSpec((1, MLA_OUT, tm), lambda i: (i // (ATT_T // tm), 0, i % (ATT_T // tm))),
            row(ML_HEADS * ML_QK), row(ML_HEADS * ML_QK), row(ML_OUT), row(ML_OUT),
            pl.BlockSpec((8, tm), lambda i: (0, i)),
        ),
        out_shape=out_shapes,
        scratch_shapes=[pltpu.VMEM((tm, D_MODEL), BF16)],
        compiler_params=pltpu.CompilerParams(
            dimension_semantics=("arbitrary",), vmem_limit_bytes=V7X_VMEM_LIMIT_BYTES),
        name="proj_mla_prep",
    )(x2, pos_col, invf, w_all, wg_t, gbias, qg, kvg, wuq_all, wuk, wuv)


def _scores(q, k):
    return _nt_dot(k, q)


def _chunk_causal(s):
    t = s.shape[0]
    kc = lax.broadcasted_iota(jnp.int32, (t, t), 0) // CHUNK
    qc = lax.broadcasted_iota(jnp.int32, (t, t), 1) // CHUNK
    return jnp.where(kc <= qc, s, -jnp.inf)


def _softmax_update(state, s, vt, smax=None):
    m_prev, l_prev, acc = state
    if smax is None:
        smax = jnp.max(s, axis=0, keepdims=True)
    m_new = jnp.maximum(m_prev, smax)
    alpha = jnp.exp2(m_prev - m_new)
    p = jnp.exp2(s - m_new)
    l_new = alpha * l_prev + jnp.sum(p, axis=0, keepdims=True)
    acc_new = alpha * acc + _dot(vt, p.astype(BF16))
    return m_new, l_new, acc_new


def _flash_kernel(q_ref, k_ref, vt_ref, beta_ref, qn_ref, kf_ref, o_ref, m_sc, l_sc, acc_sc, s_sc,
                  mx_sc):
    i = pl.program_id(2)
    t = ATT_T
    m_sc[...] = jnp.full(m_sc.shape, NEG_BIG, F32)
    l_sc[...] = jnp.zeros(l_sc.shape, F32)
    acc_sc[...] = jnp.zeros(acc_sc.shape, F32)

    def keys(j):
        return k_ref[pl.ds(pl.multiple_of(j * t, t), t), :]

    def load(half):
        return m_sc[half], l_sc[half], acc_sc[half]

    def store(half, state):
        m_sc[half], l_sc[half], acc_sc[half] = state

    def put(tile, scores):
        s_sc[tile] = scores
        mx_sc[tile] = jnp.max(scores, axis=0, keepdims=True)

    q0 = q_ref[0:t, :]
    q1 = q_ref[t:2 * t, :]

    first_step = (pl.program_id(0) == 0) & (pl.program_id(1) == 0) & (i == 0)

    @pl.when(first_step)
    def _():
        kn0, kn1 = keys(0), keys(1)
        put(0, _scores(q0, kn0))
        put(1, _scores(q1, kn0))
        put(2, _scores(q0, kn1))
        put(3, _scores(q1, kn1))

    def body(jj, carry):
        v0, v1 = vt_ref[2 * jj], vt_ref[2 * jj + 1]
        kn0, kn1 = keys(2 * jj + 2), keys(2 * jj + 3)
        sn0 = _scores(q0, kn0)
        sn1 = _scores(q1, kn0)
        st0 = _softmax_update(load(0), s_sc[0], v0, mx_sc[0])
        put(0, sn0)
        st1 = _softmax_update(load(1), s_sc[1], v0, mx_sc[1])
        put(1, sn1)
        sn2 = _scores(q0, kn1)
        store(0, _softmax_update(st0, s_sc[2], v1, mx_sc[2]))
        put(2, sn2)
        store(1, _softmax_update(st1, s_sc[3], v1, mx_sc[3]))
        put(3, _scores(q1, kn1))
        return carry

    lax.fori_loop(0, i, body, 0)
    v0, v1 = vt_ref[2 * i], vt_ref[2 * i + 1]
    qn0, qn1 = qn_ref[0:t, :], qn_ref[t:2 * t, :]
    kf0, kf1 = kf_ref[0:t, :], kf_ref[t:2 * t, :]
    sn0 = _scores(qn0, kf0)
    sn1 = _scores(qn1, kf0)
    st0 = _softmax_update(load(0), _chunk_causal(s_sc[0]), v0)
    put(0, sn0)
    st1 = _softmax_update(load(1), s_sc[1], v0, mx_sc[1])
    put(1, sn1)
    st1 = _softmax_update(st1, _chunk_causal(s_sc[3]), v1)
    sn2 = _scores(qn0, kf1)
    sn3 = _scores(qn1, kf1)
    for half, (_, l_fin, acc_fin) in enumerate((st0, st1)):
        out = jnp.transpose(acc_fin / l_fin) * beta_ref[...]
        o_ref[half * t:(half + 1) * t, :] = out.astype(BF16)
    put(2, sn2)
    put(3, sn3)


def _flash_call(q_pad, k_pad, vt, beta_mla, batch, seq):
    T = q_pad.shape[0]
    t = ATT_T
    tq = 2 * t
    nq = seq // tq
    nkv = seq // t
    n_steps = batch * MLA_HEADS * nq

    def successor(b, h, i):
        g = jnp.minimum((b * MLA_HEADS + h) * nq + i + 1, n_steps - 1)
        return g // (nq * MLA_HEADS), (g // nq) % MLA_HEADS, g % nq

    def next_q(b, h, i):
        b2, h2, i2 = successor(b, h, i)
        return b2 * nq + i2, h2

    def next_first_keys(b, h, i):
        b2, h2, _ = successor(b, h, i)
        return b2 * (seq // tq), h2

    return pl.pallas_call(
        _flash_kernel,
        grid=(batch, MLA_HEADS, nq),
        in_specs=[
            pl.BlockSpec((tq, HEAD_PAD), lambda b, h, i: (b * nq + i, h)),
            pl.BlockSpec((seq, HEAD_PAD), lambda b, h, i: (b, h)),
            pl.BlockSpec((nkv, MLA_V, t), lambda b, h, i: (b, h, 0)),
            pl.BlockSpec((1, MLA_V), lambda b, h, i: (0, h)),
            pl.BlockSpec((tq, HEAD_PAD), next_q),
            pl.BlockSpec((tq, HEAD_PAD), next_first_keys),
        ],
        out_specs=pl.BlockSpec((tq, MLA_V), lambda b, h, i: (b * nq + i, h)),
        out_shape=jax.ShapeDtypeStruct((T, MLA_OUT), BF16),
        scratch_shapes=[pltpu.VMEM((2, 1, t), F32), pltpu.VMEM((2, 1, t), F32),
                        pltpu.VMEM((2, MLA_V, t), F32), pltpu.VMEM((4, t, t), F32),
                        pltpu.VMEM((4, 1, t), F32)],
        compiler_params=pltpu.CompilerParams(
            dimension_semantics=("arbitrary", "arbitrary", "arbitrary"),
            vmem_limit_bytes=V7X_VMEM_LIMIT_BYTES),
        name="mla_flash",
    )(q_pad, k_pad, vt, beta_mla, q_pad, k_pad)


def _lane_cumsum(x):
    n = x.shape[-1]
    lane = lax.broadcasted_iota(jnp.int32, x.shape, x.ndim - 1)
    d = 1
    while d < n:
        x = x + jnp.where(lane >= d, pltpu.roll(x, d, x.ndim - 1), 0.0)
        d *= 2
    return x


def _mlstm_kernel(q_ref, k_ref, v_ref, o_ref, gt_ref, hg_ref, beta_ref, out_ref,
                  c_sc, n_sc, m_sc):
    L = ML_L

    @pl.when(pl.program_id(1) == 0)
    def _():
        c_sc[...] = jnp.zeros(c_sc.shape, F32)
        n_sc[...] = jnp.zeros(n_sc.shape, F32)
        m_sc[...] = jnp.full(m_sc.shape, -jnp.inf, F32)

    g8 = gt_ref[...]
    capped = GATE_CAP * jnp.tanh(g8 / GATE_CAP)
    logf = jnp.minimum(capped, 0.0) - jnp.log1p(jnp.exp(-jnp.abs(capped)))
    row8 = lax.broadcasted_iota(jnp.int32, g8.shape, 0)
    bsum = _lane_cumsum(jnp.where(row8 >= ML_HEADS, logf, 0.0))
    gb8 = jnp.where(row8 < ML_HEADS, capped - pltpu.roll(bsum, ML_HEADS, 0), bsum)

    rr = lax.broadcasted_iota(jnp.int32, (L, L), 0)
    cc = lax.broadcasted_iota(jnp.int32, (L, L), 1)
    causal = cc <= rr
    eye = cc == rr

    for h in range(ML_HEADS):
        g_row = gb8[h:h + 1, :]
        b_row = gb8[ML_HEADS + h:ML_HEADS + h + 1, :]
        G = jnp.broadcast_to(g_row, (L, L))
        Bm = jnp.broadcast_to(b_row, (L, L))
        m_prev = m_sc[h:h + 1, 0:1]
        cummax = jnp.max(jnp.where(causal, G, -jnp.inf), axis=1, keepdims=True)
        mcol = jnp.maximum(m_prev, cummax)
        g_col = jnp.sum(jnp.where(eye, G, 0.0), axis=1, keepdims=True)
        b_col = jnp.sum(jnp.where(eye, Bm, 0.0), axis=1, keepdims=True)
        dmat = jnp.where(causal, jnp.exp(G - mcol), 0.0)
        inter_w = jnp.exp(m_prev - mcol)

        qh = q_ref[:, h * ML_QK:(h + 1) * ML_QK]
        kh = k_ref[:, h * ML_QK:(h + 1) * ML_QK]
        vh = v_ref[:, h * ML_V:(h + 1) * ML_V]
        ct = c_sc[h]
        nrow = n_sc[h:h + 1, :]

        s = _nt_dot(qh, kh) * dmat
        num = _dot(s.astype(BF16), vh) + inter_w * _dot(qh, ct.astype(BF16))
        den = (jnp.sum(s, axis=1, keepdims=True)
               + inter_w * jnp.sum(qh.astype(F32) * nrow, axis=1, keepdims=True))
        hval = num / jnp.maximum(jnp.abs(den), jnp.exp(-(b_col + mcol)))

        m_last = mcol[L - 1:L, :]
        decay = inter_w[L - 1:L, :]
        w_col = jnp.exp(g_col - m_last)
        kw = kh.astype(F32) * w_col
        upd = lax.dot_general(kw.astype(BF16), vh, (((0,), (0,)), ((), ())),
                              preferred_element_type=F32)
        c_sc[h] = decay * ct + upd
        n_sc[h:h + 1, :] = decay * nrow + jnp.sum(kw, axis=0, keepdims=True)
        m_sc[h:h + 1, :] = jnp.broadcast_to(b_row[:, L - 1:L] + m_last, (1, LANES))

        hn = hval * lax.rsqrt(jnp.mean(hval * hval, axis=-1, keepdims=True) + RMS_EPS)
        hn = hn * hg_ref[:, h * ML_V:(h + 1) * ML_V]
        og = jax.nn.sigmoid(o_ref[:, h * ML_V:(h + 1) * ML_V].astype(F32))
        out_ref[:, h * ML_V:(h + 1) * ML_V] = (
            og * hn * beta_ref[:, h * ML_V:(h + 1) * ML_V]).astype(BF16)


def _mlstm_call(mq, mk, mv, mo, gates_t, head_g, beta_ml, batch, seq):
    T = mq.shape[0]
    L = ML_L
    nc = seq // L
    blk = lambda w: pl.BlockSpec((L, w), lambda b, c: (b * nc + c, 0))
    return pl.pallas_call(
        _mlstm_kernel,
        grid=(batch, nc),
        in_specs=[
            blk(ML_HEADS * ML_QK), blk(ML_HEADS * ML_QK), blk(ML_OUT), blk(ML_OUT),
            pl.BlockSpec((8, L), lambda b, c: (0, b * nc + c)),
            _const_spec((1, ML_OUT)),
            _const_spec((1, ML_OUT)),
        ],
        out_specs=blk(ML_OUT),
        out_shape=jax.ShapeDtypeStruct((T, ML_OUT), BF16),
        scratch_shapes=[pltpu.VMEM((ML_HEADS, ML_QK, ML_V), F32),
                        pltpu.VMEM((8, ML_QK), F32),
                        pltpu.VMEM((8, LANES), F32)],
        compiler_params=pltpu.CompilerParams(
            dimension_semantics=("arbitrary", "arbitrary"),
            vmem_limit_bytes=V7X_VMEM_LIMIT_BYTES),
        name="mlstm_scan",
    )(mq, mk, mv, mo, gates_t, head_g, beta_ml)


def _layernorm(y, g, b):
    mu = jnp.mean(y, axis=-1, keepdims=True)
    yc = y - mu
    return yc * lax.rsqrt(jnp.mean(yc * yc, axis=-1, keepdims=True) + LN_EPS) * g + b


def _outproj_kernel(a_ref, m_ref, x_ref, w_ref, g_ref, b_ref, o_ref):
    tm = o_ref.shape[0]
    sub = tm // OUT_SUBTILES

    def rows(r):
        return slice(r * sub, (r + 1) * sub)

    def project(r):
        return _dot(a_ref[rows(r), :], w_ref[0:MLA_OUT, :]) + _dot(m_ref[rows(r), :], w_ref[MLA_OUT:, :])

    def finish(r, acc):
        o_ref[rows(r), :] = _layernorm(DN_ALPHA * x_ref[rows(r), :] + acc, g_ref[...], b_ref[...])

    acc = project(0)
    for r in range(1, OUT_SUBTILES):
        nxt = project(r)
        finish(r - 1, acc)
        acc = nxt
    finish(OUT_SUBTILES - 1, acc)


def _outproj_call(attn, mem, x2, w_out, g, b):
    T = x2.shape[0]
    tm = OUT_TM
    return pl.pallas_call(
        _outproj_kernel,
        grid=(T // tm,),
        in_specs=[
            pl.BlockSpec((tm, MLA_OUT), lambda i: (i, 0)),
            pl.BlockSpec((tm, ML_OUT), lambda i: (i, 0)),
            pl.BlockSpec((tm, D_MODEL), lambda i: (i, 0)),
            _const_spec((MLA_OUT + ML_OUT, D_MODEL)),
            _const_spec((1, D_MODEL)),
            _const_spec((1, D_MODEL)),
        ],
        out_specs=pl.BlockSpec((tm, D_MODEL), lambda i: (i, 0)),
        out_shape=jax.ShapeDtypeStruct((T, D_MODEL), F32),
        compiler_params=pltpu.CompilerParams(
            dimension_semantics=("arbitrary",), vmem_limit_bytes=V7X_VMEM_LIMIT_BYTES),
        name="outproj_ln",
    )(attn, mem, x2, w_out, g, b)


def _gelu_tanh(x):
    c = math.sqrt(2.0 / math.pi)
    return 0.5 * x * (1.0 + jnp.tanh(c * (x + 0.044715 * (x * x * x))))


def _ffn_kernel(tiles_per_seq, h_ref, wgv_ref, wd_ref, cw_ref, cb_ref, g_ref, b_ref,
                o_ref, hb_ref, gbuf_ref, val_ref, halo_ref):
    i = pl.program_id(0)
    f = pl.program_id(1)
    nf = pl.num_programs(1) - 1
    tm = FFN_TM
    cols = [slice(c * FFN_SUB, (c + 1) * FFN_SUB) for c in range(FFN_TF // FFN_SUB)]
    first = (i % tiles_per_seq) == 0

    fc = jnp.maximum(f - 1, 0)

    def up_project(c):
        hb = hb_ref[...]
        cv = slice(FFN_TF + c.start, FFN_TF + c.stop)
        return _dot(hb, wgv_ref[:, c]), _dot(hb, wgv_ref[:, cv])

    def stash(c, gate, val):
        gbuf_ref[0:8, c] = jnp.where(first, 0.0, halo_ref[f, :, c])
        gbuf_ref[8:8 + tm, c] = gate
        halo_ref[f, :, c] = gate[tm - 8:tm, :]
        val_ref[:, c] = val

    def down_project(c):
        cw = cw_ref[fc]
        conv = (cw[0:1, c] * gbuf_ref[6:6 + tm, c] + cw[1:2, c] * gbuf_ref[7:7 + tm, c]
                + cw[2:3, c] * gbuf_ref[8:8 + tm, c] + cb_ref[fc][:, c])
        act = (_gelu_tanh(conv) * val_ref[:, c]).astype(BF16)
        return _dot(act, wd_ref[c, :])

    def consume():
        acc = None
        for c in cols:
            part = down_project(c)
            acc = part if acc is None else acc + part
        o_ref[...] += acc

    @pl.when(jnp.logical_and(i == 0, f == 0))
    def _():
        halo_ref[...] = jnp.zeros(halo_ref.shape, F32)

    @pl.when(f == 0)
    def _():
        hb_ref[...] = h_ref[...].astype(BF16)
        o_ref[...] = DN_ALPHA * h_ref[...]
        for c in cols:
            stash(c, *up_project(c))

    @pl.when(jnp.logical_and(f > 0, f < nf))
    def _():
        ups = [up_project(c) for c in cols]
        consume()
        for c, (gate, val) in zip(cols, ups):
            stash(c, gate, val)

    @pl.when(f == nf)
    def _():
        consume()
        o_ref[...] = _layernorm(o_ref[...], g_ref[...], b_ref[...])


def _ffn_call(h1, wgv, wd, conv_w, conv_b, g, b, seq):
    T = h1.shape[0]
    tm, tf = FFN_TM, FFN_TF
    nf = D_FF // tf
    kern = functools.partial(_ffn_kernel, seq // tm)
    produced = lambda i, f: (0, jnp.minimum(f, nf - 1))
    return pl.pallas_call(
        kern,
        grid=(T // tm, nf + 1),
        in_specs=[
            pl.BlockSpec((tm, D_MODEL), lambda i, f: (i, 0)),
            pl.BlockSpec((D_MODEL, 2 * tf), produced),
            pl.BlockSpec((tf, D_MODEL), lambda i, f: (jnp.maximum(f - 1, 0), 0)),
            _const_spec((nf, 3, tf)),
            _const_spec((nf, 1, tf)),
            _const_spec((1, D_MODEL)),
            _const_spec((1, D_MODEL)),
        ],
        out_specs=pl.BlockSpec((tm, D_MODEL), lambda i, f: (i, 0)),
        out_shape=jax.ShapeDtypeStruct((T, D_MODEL), F32),
        scratch_shapes=[pltpu.VMEM((tm, D_MODEL), BF16),
                        pltpu.VMEM((tm + 8, tf), F32),
                        pltpu.VMEM((tm, tf), F32),
                        pltpu.VMEM((nf, 8, tf), F32)],
        compiler_params=pltpu.CompilerParams(
            dimension_semantics=("arbitrary", "arbitrary"),
            vmem_limit_bytes=V7X_VMEM_LIMIT_BYTES),
        name="convglu_ffn_ln",
    )(h1, wgv, wd, conv_w, conv_b, g, b)


def _prep_proj_weights(w_in, w_uq, b_igate, b_fgate):
    half = MLA_ROPE // 2
    o = 0
    w_cq = w_in[:, o:o + Q_LORA]; o += Q_LORA
    w_ckv = w_in[:, o:o + KV_LORA]; o += KV_LORA
    w_kr = w_in[:, o:o + MLA_ROPE]; o += MLA_ROPE
    w_mq = w_in[:, o:o + ML_HEADS * ML_QK]; o += ML_HEADS * ML_QK
    w_mk = w_in[:, o:o + ML_HEADS * ML_QK]; o += ML_HEADS * ML_QK
    w_mv = w_in[:, o:o + ML_OUT]; o += ML_OUT
    w_mo = w_in[:, o:o + ML_OUT]; o += ML_OUT
    w_gates = w_in[:, o:o + 2 * ML_HEADS]
    w_krot = jnp.concatenate([-w_kr[:, half:], w_kr[:, :half]], axis=1)
    z64 = jnp.zeros((D_MODEL, LANES - MLA_ROPE), w_in.dtype)
    w_all = jnp.concatenate([w_cq, w_ckv, w_kr, z64, w_krot, z64, w_mq, w_mk, w_mv, w_mo], axis=1)
    wg_t = w_gates.T
    gbias = jnp.concatenate([b_igate, b_fgate])[:, None]

    uq = w_uq.reshape(Q_LORA, MLA_HEADS, MLA_NOPE + MLA_ROPE)
    uq_nope = uq[:, :, :MLA_NOPE].reshape(Q_LORA, MLA_HEADS * MLA_NOPE)
    uq_r = uq[:, :, MLA_NOPE:]
    uq_rot = jnp.concatenate([-uq_r[:, :, half:], uq_r[:, :, :half]], axis=2)
    zpad = jnp.zeros((Q_LORA, MLA_HEADS, LANES - MLA_ROPE), w_uq.dtype)
    uq_r = jnp.concatenate([uq_r, zpad], axis=2).reshape(Q_LORA, MLA_HEADS * LANES)
    uq_rot = jnp.concatenate([uq_rot, zpad], axis=2).reshape(Q_LORA, MLA_HEADS * LANES)
    wuq_all = jnp.concatenate([uq_nope, uq_r, uq_rot], axis=1)
    return w_all.astype(BF16), wg_t.astype(BF16), gbias.astype(F32), wuq_all.astype(BF16)


def kernel(x, positions, w_in, q_norm_g, kv_norm_g, w_uq, w_uk, w_uv, b_igate, b_fgate,
           ml_head_g, beta_mla, beta_ml, w_out, ln1_g, ln1_b, w_ffn_gate, w_ffn_val,
           conv_w, conv_b, w_down, ln2_g, ln2_b):
    B, S, D = x.shape
    T = B * S
    assert D == D_MODEL and w_in.shape[0] == DEPTH == 1
    assert S % (2 * ATT_T) == 0 and S % ML_L == 0 and S % FFN_TM == 0 and T % PROJ_TM == 0
    l = 0
    x2 = x.reshape(T, D)
    pos_col = positions.reshape(T, 1).astype(F32)
    inv_freq = 1.0 / (ROPE_BASE ** (jnp.arange(0, MLA_ROPE, 2, dtype=F32) / MLA_ROPE))
    invf = jnp.tile(inv_freq, LANES // (MLA_ROPE // 2))[None, :]

    w_all, wg_t, gbias, wuq_all = _prep_proj_weights(w_in[l], w_uq[l], b_igate[l], b_fgate[l])
    q_pad, k_pad, vt, mq, mk, mv, mo, gates_t = _proj_call(
        x2, pos_col, invf, w_all, wg_t, gbias,
        q_norm_g[l][None, :], kv_norm_g[l][None, :], wuq_all,
        w_uk[l].astype(BF16), w_uv[l].T.astype(BF16))

    attn = _flash_call(q_pad, k_pad, vt, beta_mla[l][None, :], B, S)
    mem = _mlstm_call(mq, mk, mv, mo, gates_t, ml_head_g[l][None, :], beta_ml[l][None, :], B, S)
    h1 = _outproj_call(attn, mem, x2, w_out[l].astype(BF16), ln1_g[l][None, :], ln1_b[l][None, :])
    nf = D_FF // FFN_TF
    wgv = jnp.stack([w_ffn_gate[l].reshape(D, nf, FFN_TF), w_ffn_val[l].reshape(D, nf, FFN_TF)],
                    axis=2).reshape(D, 2 * D_FF).astype(BF16)
    cw3 = conv_w[l].reshape(3, nf, FFN_TF).transpose(1, 0, 2)
    cb3 = conv_b[l].reshape(nf, 1, FFN_TF)
    out = _ffn_call(h1, wgv, w_down[l].astype(BF16), cw3, cb3,
                    ln2_g[l][None, :], ln2_b[l][None, :], S)
    return out.reshape(B, S, D)
```

```python
import functools
import math

import jax
import jax.numpy as jnp
from jax import lax
from jax.experimental import pallas as pl
from jax.experimental.pallas import tpu as pltpu

D_MODEL = 2048
CHUNK = 64
MLA_HEADS = 8
MLA_NOPE = 128
MLA_ROPE = 64
MLA_V = 128
Q_LORA = 512
KV_LORA = 256
ROPE_BASE = 10000.0
MLA_OUT = MLA_HEADS * MLA_V
ML_HEADS = 4
ML_QK = 128
ML_V = 256
ML_OUT = ML_HEADS * ML_V
GATE_CAP = 15.0
D_FF = 5632
RMS_EPS = 1e-6
LN_EPS = 1e-5
DEPTH = 1
DN_ALPHA = (2 * DEPTH) ** 0.25

LANES = 128
V7X_VMEM_LIMIT_BYTES = 58 * 1024 * 1024

PROJ_TM = 256
ATT_T = 512
ML_L = 256
OUT_TM = 512
OUT_SUBTILES = 4
FFN_TM = 512
FFN_TF = 512
FFN_SUB = 256

HEAD_PAD = 2 * LANES
PROJ_W = 4096
_C_CQ, _C_CKV, _C_KR, _C_KROT, _C_MQ, _C_MK, _C_MV, _C_MO = 0, 512, 768, 896, 1024, 1536, 2048, 3072

F32 = jnp.float32
BF16 = jnp.bfloat16
NEG_BIG = -1e30


def _nt_dot(a, b):
    return lax.dot_general(a, b, (((1,), (1,)), ((), ())), preferred_element_type=F32)


def _dot(a, b):
    return jnp.dot(a, b, preferred_element_type=F32)


def _const_spec(shape):
    nd = len(shape)
    return pl.BlockSpec(shape, lambda *_: (0,) * nd, pipeline_mode=pl.Buffered(1))


def _proj_kernel(x_ref, pos_ref, invf_ref, w_ref, wg_ref, gb_ref, qg_ref, kvg_ref,
                 wuq_ref, wuk_ref, wuv_ref,
                 q_ref, k_ref, vt_ref, mq_ref, mk_ref, mv_ref, mo_ref, gt_ref,
                 xb_ref):
    xb_ref[...] = x_ref[...].astype(BF16)
    xb = xb_ref[...]
    tm = xb.shape[0]

    cq = _dot(xb, w_ref[:, _C_CQ:_C_CQ + Q_LORA])
    p1 = _dot(xb, w_ref[:, _C_CKV:_C_MQ])
    mq_ref[...] = _dot(xb, w_ref[:, _C_MQ:_C_MK]).astype(BF16)
    mk_ref[...] = (_dot(xb, w_ref[:, _C_MK:_C_MV]) * (ML_QK ** -0.5)).astype(BF16)

    ang = pos_ref[...] * invf_ref[...]
    cos = jnp.cos(ang)
    sin = jnp.sin(ang)

    cq = cq * lax.rsqrt(jnp.mean(cq * cq, axis=-1, keepdims=True) + RMS_EPS) * qg_ref[...]
    cqb = cq.astype(BF16)
    ckv = p1[:, 0:KV_LORA]
    ckv = ckv * lax.rsqrt(jnp.mean(ckv * ckv, axis=-1, keepdims=True) + RMS_EPS) * kvg_ref[...]
    ckvb = ckv.astype(BF16)
    k_rope = (p1[:, 256:384] * cos + p1[:, 384:512] * sin).astype(BF16)

    qscale = (MLA_NOPE + MLA_ROPE) ** -0.5 * math.log2(math.e)
    qt_nope = _nt_dot(wuq_ref[0:1024, :], cqb) * qscale
    qt_r = _nt_dot(wuq_ref[1024:2048, :], cqb)
    qt_rot = _nt_dot(wuq_ref[2048:3072, :], cqb)
    k_nope = _dot(ckvb, wuk_ref[...]).astype(BF16)
    vt_ref[0] = _nt_dot(wuv_ref[...], ckvb).astype(BF16)
    mv_ref[...] = _dot(xb, w_ref[:, _C_MV:_C_MO]).astype(BF16)
    mo_ref[...] = _dot(xb, w_ref[:, _C_MO:PROJ_W]).astype(BF16)
    gt_ref[...] = _nt_dot(wg_ref[...], xb) + gb_ref[...]

    cos_t = jnp.transpose(cos)
    sin_t = jnp.transpose(sin)
    for h in range(MLA_HEADS):
        rows = slice(h * LANES, (h + 1) * LANES)
        q_ref[h * HEAD_PAD:h * HEAD_PAD + LANES, :] = qt_nope[rows, :].astype(BF16)
        q_ref[h * HEAD_PAD + LANES:(h + 1) * HEAD_PAD, :] = (
            (qt_r[rows, :] * cos_t + qt_rot[rows, :] * sin_t) * qscale).astype(BF16)
        k_ref[:, h * HEAD_PAD:h * HEAD_PAD + LANES] = k_nope[:, h * LANES:(h + 1) * LANES]
        k_ref[:, h * HEAD_PAD + LANES:(h + 1) * HEAD_PAD] = k_rope


def _proj_call(x2, pos_col, invf, w_all, wg_t, gbias, qg, kvg, wuq_all, wuk, wuv):
    T = x2.shape[0]
    tm = PROJ_TM
    row = lambda w: pl.BlockSpec((tm, w), lambda i: (i, 0))
    out_shapes = (
        jax.ShapeDtypeStruct((MLA_HEADS * HEAD_PAD, T), BF16),
        jax.ShapeDtypeStruct((T, MLA_HEADS * HEAD_PAD), BF16),
        jax.ShapeDtypeStruct((T // ATT_T, MLA_OUT, ATT_T), BF16),
        jax.ShapeDtypeStruct((T, ML_HEADS * ML_QK), BF16),
        jax.ShapeDtypeStruct((T, ML_HEADS * ML_QK), BF16),
        jax.ShapeDtypeStruct((T, ML_OUT), BF16),
        jax.ShapeDtypeStruct((T, ML_OUT), BF16),
        jax.ShapeDtypeStruct((8, T), F32),
    )
    return pl.pallas_call(
        _proj_kernel,
        grid=(T // tm,),
        in_specs=[
            row(D_MODEL),
            pl.BlockSpec((tm, 1), lambda i: (i, 0)),
            _const_spec((1, LANES)),
            _const_spec((D_MODEL, PROJ_W)),
            _const_spec((8, D_MODEL)),
            _const_spec((8, 1)),
            _const_spec((1, Q_LORA)),
            _const_spec((1, KV_LORA)),
            _const_spec((3072, Q_LORA)),
            _const_spec((KV_LORA, MLA_HEADS * MLA_NOPE)),
            _const_spec((MLA_OUT, KV_LORA)),
        ],
        out_specs=(
            pl.BlockSpec((MLA_HEADS * HEAD_PAD, tm), lambda i: (0, i)),
            row(MLA_HEADS * HEAD_PAD),
            pl.BlockSpec((1, MLA_OUT, tm), lambda i: (i // (ATT_T // tm), 0, i % (ATT_T // tm))),
            row(ML_HEADS * ML_QK), row(ML_HEADS * ML_QK), row(ML_OUT), row(ML_OUT),
            pl.BlockSpec((8, tm), lambda i: (0, i)),
        ),
        out_shape=out_shapes,
        scratch_shapes=[pltpu.VMEM((tm, D_MODEL), BF16)],
        compiler_params=pltpu.CompilerParams(
            dimension_semantics=("arbitrary",), vmem_limit_bytes=V7X_VMEM_LIMIT_BYTES),
        name="proj_mla_prep",
    )(x2, pos_col, invf, w_all, wg_t, gbias, qg, kvg, wuq_all, wuk, wuv)


def _scores(qt, k):
    return _dot(k, qt)


def _chunk_causal(s):
    t = s.shape[0]
    kc = lax.broadcasted_iota(jnp.int32, (t, t), 0) // CHUNK
    qc = lax.broadcasted_iota(jnp.int32, (t, t), 1) // CHUNK
    return jnp.where(kc <= qc, s, -jnp.inf)


def _softmax_update(state, s, vt, smax=None):
    m_prev, l_prev, acc = state
    if smax is None:
        smax = jnp.max(s, axis=0, keepdims=True)
    m_new = jnp.maximum(m_prev, smax)
    alpha = jnp.exp2(m_prev - m_new)
    p = jnp.exp2(s - m_new)
    l_new = alpha * l_prev + jnp.sum(p, axis=0, keepdims=True)
    acc_new = alpha * acc + _dot(vt, p.astype(BF16))
    return m_new, l_new, acc_new


def _flash_kernel(q_ref, k_ref, vt_ref, beta_ref, qn_ref, kf_ref, o_ref, m_sc, l_sc, acc_sc, s_sc,
                  mx_sc):
    i = pl.program_id(2)
    t = ATT_T
    m_sc[...] = jnp.full(m_sc.shape, NEG_BIG, F32)
    l_sc[...] = jnp.zeros(l_sc.shape, F32)
    acc_sc[...] = jnp.zeros(acc_sc.shape, F32)

    def keys(j):
        return k_ref[pl.ds(pl.multiple_of(j * t, t), t), :]

    def load(half):
        return m_sc[half], l_sc[half], acc_sc[half]

    def store(half, state):
        m_sc[half], l_sc[half], acc_sc[half] = state

    def put(tile, scores):
        s_sc[tile] = scores
        mx_sc[tile] = jnp.max(scores, axis=0, keepdims=True)

    q0 = q_ref[:, 0:t]
    q1 = q_ref[:, t:2 * t]

    first_step = (pl.program_id(0) == 0) & (pl.program_id(1) == 0) & (i == 0)

    @pl.when(first_step)
    def _():
        kn0, kn1 = keys(0), keys(1)
        put(0, _scores(q0, kn0))
        put(1, _scores(q1, kn0))
        put(2, _scores(q0, kn1))
        put(3, _scores(q1, kn1))

    def body(jj, carry):
        v0, v1 = vt_ref[2 * jj], vt_ref[2 * jj + 1]
        kn0, kn1 = keys(2 * jj + 2), keys(2 * jj + 3)
        sn0 = _scores(q0, kn0)
        sn1 = _scores(q1, kn0)
        st0 = _softmax_update(load(0), s_sc[0], v0, mx_sc[0])
        put(0, sn0)
        st1 = _softmax_update(load(1), s_sc[1], v0, mx_sc[1])
        put(1, sn1)
        sn2 = _scores(q0, kn1)
        store(0, _softmax_update(st0, s_sc[2], v1, mx_sc[2]))
        put(2, sn2)
        store(1, _softmax_update(st1, s_sc[3], v1, mx_sc[3]))
        put(3, _scores(q1, kn1))
        return carry

    def body2(u, carry):
        body(2 * u, carry)
        return body(2 * u + 1, carry)

    lax.fori_loop(0, i // 2, body2, 0)

    @pl.when(i % 2 == 1)
    def _():
        body(i - 1, 0)

    v0, v1 = vt_ref[2 * i], vt_ref[2 * i + 1]
    qn0, qn1 = qn_ref[:, 0:t], qn_ref[:, t:2 * t]
    kf0, kf1 = kf_ref[0:t, :], kf_ref[t:2 * t, :]
    sn0 = _scores(qn0, kf0)
    sn1 = _scores(qn1, kf0)
    st0 = _softmax_update(load(0), _chunk_causal(s_sc[0]), v0)
    put(0, sn0)
    st1 = _softmax_update(load(1), s_sc[1], v0, mx_sc[1])
    put(1, sn1)
    st1 = _softmax_update(st1, _chunk_causal(s_sc[3]), v1)
    sn2 = _scores(qn0, kf1)
    sn3 = _scores(qn1, kf1)
    for half, (_, l_fin, acc_fin) in enumerate((st0, st1)):
        out = jnp.transpose(acc_fin / l_fin) * beta_ref[...]
        o_ref[half * t:(half + 1) * t, :] = out.astype(BF16)
    put(2, sn2)
    put(3, sn3)


def _flash_call(qt_pad, k_pad, vt, beta_mla, batch, seq):
    T = k_pad.shape[0]
    t = ATT_T
    tq = 2 * t
    nq = seq // tq
    nkv = seq // t
    n_steps = batch * MLA_HEADS * nq

    def successor(b, h, i):
        g = jnp.minimum((b * MLA_HEADS + h) * nq + i + 1, n_steps - 1)
        return g // (nq * MLA_HEADS), (g // nq) % MLA_HEADS, g % nq

    def next_q(b, h, i):
        b2, h2, i2 = successor(b, h, i)
        return h2, b2 * nq + i2

    def next_first_keys(b, h, i):
        b2, h2, _ = successor(b, h, i)
        return b2 * (seq // tq), h2

    return pl.pallas_call(
        _flash_kernel,
        grid=(batch, MLA_HEADS, nq),
        in_specs=[
            pl.BlockSpec((HEAD_PAD, tq), lambda b, h, i: (h, b * nq + i)),
            pl.BlockSpec((seq, HEAD_PAD), lambda b, h, i: (b, h)),
            pl.BlockSpec((nkv, MLA_V, t), lambda b, h, i: (b, h, 0)),
            pl.BlockSpec((1, MLA_V), lambda b, h, i: (0, h)),
            pl.BlockSpec((HEAD_PAD, tq), next_q),
            pl.BlockSpec((tq, HEAD_PAD), next_first_keys),
        ],
        out_specs=pl.BlockSpec((tq, MLA_V), lambda b, h, i: (b * nq + i, h)),
        out_shape=jax.ShapeDtypeStruct((T, MLA_OUT), BF16),
        scratch_shapes=[pltpu.VMEM((2, 1, t), F32), pltpu.VMEM((2, 1, t), F32),
                        pltpu.VMEM((2, MLA_V, t), F32), pltpu.VMEM((4, t, t), F32),
                        pltpu.VMEM((4, 1, t), F32)],
        compiler_params=pltpu.CompilerParams(
            dimension_semantics=("arbitrary", "arbitrary", "arbitrary"),
            vmem_limit_bytes=V7X_VMEM_LIMIT_BYTES),
        name="mla_flash",
    )(qt_pad, k_pad, vt, beta_mla, qt_pad, k_pad)


def _lane_cumsum(x):
    n = x.shape[-1]
    lane = lax.broadcasted_iota(jnp.int32, x.shape, x.ndim - 1)
    d = 1
    while d < n:
        x = x + jnp.where(lane >= d, pltpu.roll(x, d, x.ndim - 1), 0.0)
        d *= 2
    return x


def _mlstm_kernel(q_ref, k_ref, v_ref, o_ref, gt_ref, hg_ref, beta_ref, out_ref,
                  c_sc, n_sc, m_sc):
    L = ML_L

    @pl.when(pl.program_id(1) == 0)
    def _():
        c_sc[...] = jnp.zeros(c_sc.shape, F32)
        n_sc[...] = jnp.zeros(n_sc.shape, F32)
        m_sc[...] = jnp.full(m_sc.shape, -jnp.inf, F32)

    g8 = gt_ref[...]
    capped = GATE_CAP * jnp.tanh(g8 / GATE_CAP)
    logf = jnp.minimum(capped, 0.0) - jnp.log1p(jnp.exp(-jnp.abs(capped)))
    row8 = lax.broadcasted_iota(jnp.int32, g8.shape, 0)
    bsum = _lane_cumsum(jnp.where(row8 >= ML_HEADS, logf, 0.0))
    gb8 = jnp.where(row8 < ML_HEADS, capped - pltpu.roll(bsum, ML_HEADS, 0), bsum)

    rr = lax.broadcasted_iota(jnp.int32, (L, L), 0)
    cc = lax.broadcasted_iota(jnp.int32, (L, L), 1)
    causal = cc <= rr
    eye = cc == rr

    for h in range(ML_HEADS):
        g_row = gb8[h:h + 1, :]
        b_row = gb8[ML_HEADS + h:ML_HEADS + h + 1, :]
        G = jnp.broadcast_to(g_row, (L, L))
        Bm = jnp.broadcast_to(b_row, (L, L))
        m_prev = m_sc[h:h + 1, 0:1]
        cummax = jnp.max(jnp.where(causal, G, -jnp.inf), axis=1, keepdims=True)
        mcol = jnp.maximum(m_prev, cummax)
        g_col = jnp.sum(jnp.where(eye, G, 0.0), axis=1, keepdims=True)
        b_col = jnp.sum(jnp.where(eye, Bm, 0.0), axis=1, keepdims=True)
        dmat = jnp.where(causal, jnp.exp(G - mcol), 0.0)
        inter_w = jnp.exp(m_prev - mcol)

        qh = q_ref[:, h * ML_QK:(h + 1) * ML_QK]
        kh = k_ref[:, h * ML_QK:(h + 1) * ML_QK]
        vh = v_ref[:, h * ML_V:(h + 1) * ML_V]
        ct = c_sc[h]
        nrow = n_sc[h:h + 1, :]

        s = _nt_dot(qh, kh) * dmat
        num = _dot(s.astype(BF16), vh) + inter_w * _dot(qh, ct.astype(BF16))
        den = (jnp.sum(s, axis=1, keepdims=True)
               + inter_w * jnp.sum(qh.astype(F32) * nrow, axis=1, keepdims=True))
        hval = num / jnp.maximum(jnp.abs(den), jnp.exp(-(b_col + mcol)))

        m_last = mcol[L - 1:L, :]
        decay = inter_w[L - 1:L, :]
        w_col = jnp.exp(g_col - m_last)
        kw = kh.astype(F32) * w_col
        upd = lax.dot_general(kw.astype(BF16), vh, (((0,), (0,)), ((), ())),
                              preferred_element_type=F32)
        c_sc[h] = decay * ct + upd
        n_sc[h:h + 1, :] = decay * nrow + jnp.sum(kw, axis=0, keepdims=True)
        m_sc[h:h + 1, :] = jnp.broadcast_to(b_row[:, L - 1:L] + m_last, (1, LANES))

        hn = hval * lax.rsqrt(jnp.mean(hval * hval, axis=-1, keepdims=True) + RMS_EPS)
        hn = hn * hg_ref[:, h * ML_V:(h + 1) * ML_V]
        og = jax.nn.sigmoid(o_ref[:, h * ML_V:(h + 1) * ML_V].astype(F32))
        out_ref[:, h * ML_V:(h + 1) * ML_V] = (
            og * hn * beta_ref[:, h * ML_V:(h + 1) * ML_V]).astype(BF16)


def _mlstm_call(mq, mk, mv, mo, gates_t, head_g, beta_ml, batch, seq):
    T = mq.shape[0]
    L = ML_L
    nc = seq // L
    blk = lambda w: pl.BlockSpec((L, w), lambda b, c: (b * nc + c, 0))
    return pl.pallas_call(
        _mlstm_kernel,
        grid=(batch, nc),
        in_specs=[
            blk(ML_HEADS * ML_QK), blk(ML_HEADS * ML_QK), blk(ML_OUT), blk(ML_OUT),
            pl.BlockSpec((8, L), lambda b, c: (0, b * nc + c)),
            _const_spec((1, ML_OUT)),
            _const_spec((1, ML_OUT)),
        ],
        out_specs=blk(ML_OUT),
        out_shape=jax.ShapeDtypeStruct((T, ML_OUT), BF16),
        scratch_shapes=[pltpu.VMEM((ML_HEADS, ML_QK, ML_V), F32),
                        pltpu.VMEM((8, ML_QK), F32),
                        pltpu.VMEM((8, LANES), F32)],
        compiler_params=pltpu.CompilerParams(
            dimension_semantics=("arbitrary", "arbitrary"),
            vmem_limit_bytes=V7X_VMEM_LIMIT_BYTES),
        name="mlstm_scan",
    )(mq, mk, mv, mo, gates_t, head_g, beta_ml)


def _layernorm(y, g, b):
    mu = jnp.mean(y, axis=-1, keepdims=True)
    yc = y - mu
    return yc * lax.rsqrt(jnp.mean(yc * yc, axis=-1, keepdims=True) + LN_EPS) * g + b


def _outproj_kernel(a_ref, m_ref, x_ref, w_ref, g_ref, b_ref, o_ref):
    tm = o_ref.shape[0]
    sub = tm // OUT_SUBTILES

    def rows(r):
        return slice(r * sub, (r + 1) * sub)

    def project(r):
        return _dot(a_ref[rows(r), :], w_ref[0:MLA_OUT, :]) + _dot(m_ref[rows(r), :], w_ref[MLA_OUT:, :])

    def finish(r, acc):
        o_ref[rows(r), :] = _layernorm(DN_ALPHA * x_ref[rows(r), :] + acc, g_ref[...], b_ref[...])

    acc = project(0)
    for r in range(1, OUT_SUBTILES):
        nxt = project(r)
        finish(r - 1, acc)
        acc = nxt
    finish(OUT_SUBTILES - 1, acc)


def _outproj_call(attn, mem, x2, w_out, g, b):
    T = x2.shape[0]
    tm = OUT_TM
    return pl.pallas_call(
        _outproj_kernel,
        grid=(T // tm,),
        in_specs=[
            pl.BlockSpec((tm, MLA_OUT), lambda i: (i, 0)),
            pl.BlockSpec((tm, ML_OUT), lambda i: (i, 0)),
            pl.BlockSpec((tm, D_MODEL), lambda i: (i, 0)),
            _const_spec((MLA_OUT + ML_OUT, D_MODEL)),
            _const_spec((1, D_MODEL)),
            _const_spec((1, D_MODEL)),
        ],
        out_specs=pl.BlockSpec((tm, D_MODEL), lambda i: (i, 0)),
        out_shape=jax.ShapeDtypeStruct((T, D_MODEL), F32),
        compiler_params=pltpu.CompilerParams(
            dimension_semantics=("arbitrary",), vmem_limit_bytes=V7X_VMEM_LIMIT_BYTES),
        name="outproj_ln",
    )(attn, mem, x2, w_out, g, b)


def _gelu_tanh(x):
    c = math.sqrt(2.0 / math.pi)
    return 0.5 * x * (1.0 + jnp.tanh(c * (x + 0.044715 * (x * x * x))))


def _ffn_kernel(tiles_per_seq, h_ref, wg_ref, wv_ref, wd_ref, cw_ref, cb_ref, g_ref, b_ref,
                o_ref, hb_ref, gbuf_ref, val_ref, halo_ref):
    i = pl.program_id(0)
    f = pl.program_id(1)
    nf = pl.num_programs(1) - 1
    tm = FFN_TM
    cols = [slice(c * FFN_SUB, (c + 1) * FFN_SUB) for c in range(FFN_TF // FFN_SUB)]
    first = (i % tiles_per_seq) == 0

    def up_project(c):
        hb = hb_ref[...]
        return _dot(hb, wg_ref[:, c]), _dot(hb, wv_ref[:, c])

    def stash(c, gate, val):
        gbuf_ref[0:8, c] = jnp.where(first, 0.0, halo_ref[f, :, c])
        gbuf_ref[8:8 + tm, c] = gate
        halo_ref[f, :, c] = gate[tm - 8:tm, :]
        val_ref[:, c] = val

    def down_project(c):
        conv = (cw_ref[0:1, c] * gbuf_ref[6:6 + tm, c] + cw_ref[1:2, c] * gbuf_ref[7:7 + tm, c]
                + cw_ref[2:3, c] * gbuf_ref[8:8 + tm, c] + cb_ref[:, c])
        act = (_gelu_tanh(conv) * val_ref[:, c]).astype(BF16)
        return _dot(act, wd_ref[c, :])

    def consume():
        acc = None
        for c in cols:
            part = down_project(c)
            acc = part if acc is None else acc + part
        o_ref[...] += acc

    @pl.when(jnp.logical_and(i == 0, f == 0))
    def _():
        halo_ref[...] = jnp.zeros(halo_ref.shape, F32)

    @pl.when(f == 0)
    def _():
        hb_ref[...] = h_ref[...].astype(BF16)
        o_ref[...] = DN_ALPHA * h_ref[...]
        for c in cols:
            stash(c, *up_project(c))

    @pl.when(jnp.logical_and(f > 0, f < nf))
    def _():
        ups = [up_project(c) for c in cols]
        consume()
        for c, (gate, val) in zip(cols, ups):
            stash(c, gate, val)

    @pl.when(f == nf)
    def _():
        consume()
        o_ref[...] = _layernorm(o_ref[...], g_ref[...], b_ref[...])


def _ffn_call(h1, wg, wv, wd, conv_w, conv_b, g, b, seq):
    T = h1.shape[0]
    tm, tf = FFN_TM, FFN_TF
    nf = D_FF // tf
    kern = functools.partial(_ffn_kernel, seq // tm)
    produced = lambda i, f: (0, jnp.minimum(f, nf - 1))
    consumed = lambda i, f: (0, jnp.maximum(f - 1, 0))
    return pl.pallas_call(
        kern,
        grid=(T // tm, nf + 1),
        in_specs=[
            pl.BlockSpec((tm, D_MODEL), lambda i, f: (i, 0)),
            pl.BlockSpec((D_MODEL, tf), produced),
            pl.BlockSpec((D_MODEL, tf), produced),
            pl.BlockSpec((tf, D_MODEL), lambda i, f: (jnp.maximum(f - 1, 0), 0)),
            pl.BlockSpec((3, tf), consumed),
            pl.BlockSpec((1, tf), consumed),
            _const_spec((1, D_MODEL)),
            _const_spec((1, D_MODEL)),
        ],
        out_specs=pl.BlockSpec((tm, D_MODEL), lambda i, f: (i, 0)),
        out_shape=jax.ShapeDtypeStruct((T, D_MODEL), F32),
        scratch_shapes=[pltpu.VMEM((tm, D_MODEL), BF16),
                        pltpu.VMEM((tm + 8, tf), F32),
                        pltpu.VMEM((tm, tf), F32),
                        pltpu.VMEM((nf, 8, tf), F32)],
        compiler_params=pltpu.CompilerParams(
            dimension_semantics=("arbitrary", "arbitrary"),
            vmem_limit_bytes=V7X_VMEM_LIMIT_BYTES),
        name="convglu_ffn_ln",
    )(h1, wg, wv, wd, conv_w, conv_b, g, b)


def _prep_proj_weights(w_in, w_uq, b_igate, b_fgate):
    half = MLA_ROPE // 2
    o = 0
    w_cq = w_in[:, o:o + Q_LORA]; o += Q_LORA
    w_ckv = w_in[:, o:o + KV_LORA]; o += KV_LORA
    w_kr = w_in[:, o:o + MLA_ROPE]; o += MLA_ROPE
    w_mq = w_in[:, o:o + ML_HEADS * ML_QK]; o += ML_HEADS * ML_QK
    w_mk = w_in[:, o:o + ML_HEADS * ML_QK]; o += ML_HEADS * ML_QK
    w_mv = w_in[:, o:o + ML_OUT]; o += ML_OUT
    w_mo = w_in[:, o:o + ML_OUT]; o += ML_OUT
    w_gates = w_in[:, o:o + 2 * ML_HEADS]
    w_krot = jnp.concatenate([-w_kr[:, half:], w_kr[:, :half]], axis=1)
    z64 = jnp.zeros((D_MODEL, LANES - MLA_ROPE), w_in.dtype)
    w_all = jnp.concatenate([w_cq, w_ckv, w_kr, z64, w_krot, z64, w_mq, w_mk, w_mv, w_mo], axis=1)
    wg_t = w_gates.T
    gbias = jnp.concatenate([b_igate, b_fgate])[:, None]

    uq = w_uq.reshape(Q_LORA, MLA_HEADS, MLA_NOPE + MLA_ROPE)
    uq_nope = uq[:, :, :MLA_NOPE].reshape(Q_LORA, MLA_HEADS * MLA_NOPE)
    uq_r = uq[:, :, MLA_NOPE:]
    uq_rot = jnp.concatenate([-uq_r[:, :, half:], uq_r[:, :, :half]], axis=2)
    zpad = jnp.zeros((Q_LORA, MLA_HEADS, LANES - MLA_ROPE), w_uq.dtype)
    uq_r = jnp.concatenate([uq_r, zpad], axis=2).reshape(Q_LORA, MLA_HEADS * LANES)
    uq_rot = jnp.concatenate([uq_rot, zpad], axis=2).reshape(Q_LORA, MLA_HEADS * LANES)
    wuq_all = jnp.concatenate([uq_nope, uq_r, uq_rot], axis=1).T
    return w_all.astype(BF16), wg_t.astype(BF16), gbias.astype(F32), wuq_all.astype(BF16)


def kernel(x, positions, w_in, q_norm_g, kv_norm_g, w_uq, w_uk, w_uv, b_igate, b_fgate,
           ml_head_g, beta_mla, beta_ml, w_out, ln1_g, ln1_b, w_ffn_gate, w_ffn_val,
           conv_w, conv_b, w_down, ln2_g, ln2_b):
    B, S, D = x.shape
    T = B * S
    assert D == D_MODEL and w_in.shape[0] == DEPTH == 1
    assert S % (2 * ATT_T) == 0 and S % ML_L == 0 and S % FFN_TM == 0 and T % PROJ_TM == 0
    l = 0
    x2 = x.reshape(T, D)
    pos_col = positions.reshape(T, 1).astype(F32)
    inv_freq = 1.0 / (ROPE_BASE ** (jnp.arange(0, MLA_ROPE, 2, dtype=F32) / MLA_ROPE))
    invf = jnp.tile(inv_freq, LANES // (MLA_ROPE // 2))[None, :]

    w_all, wg_t, gbias, wuq_all = _prep_proj_weights(w_in[l], w_uq[l], b_igate[l], b_fgate[l])
    qt_pad, k_pad, vt, mq, mk, mv, mo, gates_t = _proj_call(
        x2, pos_col, invf, w_all, wg_t, gbias,
        q_norm_g[l][None, :], kv_norm_g[l][None, :], wuq_all,
        w_uk[l].astype(BF16), w_uv[l].T.astype(BF16))

    attn = _flash_call(qt_pad, k_pad, vt, beta_mla[l][None, :], B, S)
    mem = _mlstm_call(mq, mk, mv, mo, gates_t, ml_head_g[l][None, :], beta_ml[l][None, :], B, S)
    h1 = _outproj_call(attn, mem, x2, w_out[l].astype(BF16), ln1_g[l][None, :], ln1_b[l][None, :])
    out = _ffn_call(h1, w_ffn_gate[l].astype(BF16), w_ffn_val[l].astype(BF16),
                    w_down[l].astype(BF16), conv_w[l], conv_b[l][None, :],
                    ln2_g[l][None, :], ln2_b[l][None, :], S)
    return out.reshape(B, S, D)
```

```python
import functools
import math

import jax
import jax.numpy as jnp
from jax import lax
from jax.experimental import pallas as pl
from jax.experimental.pallas import tpu as pltpu

D_MODEL = 2048
CHUNK = 64
MLA_HEADS = 8
MLA_NOPE = 128
MLA_ROPE = 64
MLA_V = 128
Q_LORA = 512
KV_LORA = 256
ROPE_BASE = 10000.0
MLA_OUT = MLA_HEADS * MLA_V
ML_HEADS = 4
ML_QK = 128
ML_V = 256
ML_OUT = ML_HEADS * ML_V
GATE_CAP = 15.0
D_FF = 5632
RMS_EPS = 1e-6
LN_EPS = 1e-5
DEPTH = 1
DN_ALPHA = (2 * DEPTH) ** 0.25

LANES = 128
V7X_VMEM_LIMIT_BYTES = 58 * 1024 * 1024

PROJ_TM = 256
ATT_T = 512
ML_L = 256
OUT_TM = 512
OUT_SUBTILES = 4
FFN_TM = 512
FFN_TF = 512
FFN_SUB = 256
FFN_LAST_SUBTILES = 2

HEAD_PAD = 2 * LANES
PROJ_W = 4096
_C_CQ, _C_CKV, _C_KR, _C_KROT, _C_MQ, _C_MK, _C_MV, _C_MO = 0, 512, 768, 896, 1024, 1536, 2048, 3072

F32 = jnp.float32
BF16 = jnp.bfloat16
NEG_BIG = -1e30


def _nt_dot(a, b):
    return lax.dot_general(a, b, (((1,), (1,)), ((), ())), preferred_element_type=F32)


def _dot(a, b):
    return jnp.dot(a, b, preferred_element_type=F32)


def _const_spec(shape):
    nd = len(shape)
    return pl.BlockSpec(shape, lambda *_: (0,) * nd, pipeline_mode=pl.Buffered(1))


def _proj_kernel(x_ref, pos_ref, invf_ref, w_ref, wg_ref, gb_ref, qg_ref, kvg_ref,
                 wuq_ref, wuk_ref, wuv_ref,
                 q_ref, k_ref, vt_ref, mq_ref, mk_ref, mv_ref, mo_ref, gt_ref,
                 xb_ref):
    xb_ref[...] = x_ref[...].astype(BF16)
    xb = xb_ref[...]
    tm = xb.shape[0]

    cq = _dot(xb, w_ref[:, _C_CQ:_C_CQ + Q_LORA])
    p1 = _dot(xb, w_ref[:, _C_CKV:_C_MQ])
    mq_ref[...] = _dot(xb, w_ref[:, _C_MQ:_C_MK]).astype(BF16)
    mk_ref[...] = (_dot(xb, w_ref[:, _C_MK:_C_MV]) * (ML_QK ** -0.5)).astype(BF16)

    ang = pos_ref[...] * invf_ref[...]
    cos = jnp.cos(ang)
    sin = jnp.sin(ang)

    cq = cq * lax.rsqrt(jnp.mean(cq * cq, axis=-1, keepdims=True) + RMS_EPS) * qg_ref[...]
    cqb = cq.astype(BF16)
    ckv = p1[:, 0:KV_LORA]
    ckv = ckv * lax.rsqrt(jnp.mean(ckv * ckv, axis=-1, keepdims=True) + RMS_EPS) * kvg_ref[...]
    ckvb = ckv.astype(BF16)
    k_rope = (p1[:, 256:384] * cos + p1[:, 384:512] * sin).astype(BF16)

    qscale = (MLA_NOPE + MLA_ROPE) ** -0.5 * math.log2(math.e)
    qt_nope = _nt_dot(wuq_ref[0:1024, :], cqb) * qscale
    qt_r = _nt_dot(wuq_ref[1024:2048, :], cqb)
    qt_rot = _nt_dot(wuq_ref[2048:3072, :], cqb)
    k_nope = _dot(ckvb, wuk_ref[...]).astype(BF16)
    vt_ref[0] = _nt_dot(wuv_ref[...], ckvb).astype(BF16)
    mv_ref[...] = _dot(xb, w_ref[:, _C_MV:_C_MO]).astype(BF16)
    mo_ref[...] = _dot(xb, w_ref[:, _C_MO:PROJ_W]).astype(BF16)
    gt_ref[...] = _nt_dot(wg_ref[...], xb) + gb_ref[...]

    cos_t = jnp.transpose(cos)
    sin_t = jnp.transpose(sin)
    for h in range(MLA_HEADS):
        rows = slice(h * LANES, (h + 1) * LANES)
        q_ref[h * HEAD_PAD:h * HEAD_PAD + LANES, :] = qt_nope[rows, :].astype(BF16)
        q_ref[h * HEAD_PAD + LANES:(h + 1) * HEAD_PAD, :] = (
            (qt_r[rows, :] * cos_t + qt_rot[rows, :] * sin_t) * qscale).astype(BF16)
        k_ref[:, h * HEAD_PAD:h * HEAD_PAD + LANES] = k_nope[:, h * LANES:(h + 1) * LANES]
        k_ref[:, h * HEAD_PAD + LANES:(h + 1) * HEAD_PAD] = k_rope


def _proj_call(x2, pos_col, invf, w_all, wg_t, gbias, qg, kvg, wuq_all, wuk, wuv):
    T = x2.shape[0]
    tm = PROJ_TM
    row = lambda w: pl.BlockSpec((tm, w), lambda i: (i, 0))
    out_shapes = (
        jax.ShapeDtypeStruct((MLA_HEADS * HEAD_PAD, T), BF16),
        jax.ShapeDtypeStruct((T, MLA_HEADS * HEAD_PAD), BF16),
        jax.ShapeDtypeStruct((T // ATT_T, MLA_OUT, ATT_T), BF16),
        jax.ShapeDtypeStruct((T, ML_HEADS * ML_QK), BF16),
        jax.ShapeDtypeStruct((T, ML_HEADS * ML_QK), BF16),
        jax.ShapeDtypeStruct((T, ML_OUT), BF16),
        jax.ShapeDtypeStruct((T, ML_OUT), BF16),
        jax.ShapeDtypeStruct((8, T), F32),
    )
    return pl.pallas_call(
        _proj_kernel,
        grid=(T // tm,),
        in_specs=[
            row(D_MODEL),
            pl.BlockSpec((tm, 1), lambda i: (i, 0)),
            _const_spec((1, LANES)),
            _const_spec((D_MODEL, PROJ_W)),
            _const_spec((8, D_MODEL)),
            _const_spec((8, 1)),
            _const_spec((1, Q_LORA)),
            _const_spec((1, KV_LORA)),
            _const_spec((3072, Q_LORA)),
            _const_spec((KV_LORA, MLA_HEADS * MLA_NOPE)),
            _const_spec((MLA_OUT, KV_LORA)),
        ],
        out_specs=(
            pl.BlockSpec((MLA_HEADS * HEAD_PAD, tm), lambda i: (0, i)),
            row(MLA_HEADS * HEAD_PAD),
            pl.BlockSpec((1, MLA_OUT, tm), lambda i: (i // (ATT_T // tm), 0, i % (ATT_T // tm))),
            row(ML_HEADS * ML_QK), row(ML_HEADS * ML_QK), row(ML_OUT), row(ML_OUT),
            pl.BlockSpec((8, tm), lambda i: (0, i)),
        ),
        out_shape=out_shapes,
        scratch_shapes=[pltpu.VMEM((tm, D_MODEL), BF16)],
        compiler_params=pltpu.CompilerParams(
            dimension_semantics=("arbitrary",), vmem_limit_bytes=V7X_VMEM_LIMIT_BYTES),
        name="proj_mla_prep",
    )(x2, pos_col, invf, w_all, wg_t, gbias, qg, kvg, wuq_all, wuk, wuv)


def _scores(qt, k):
    return _dot(k, qt)


def _chunk_causal(s):
    t = s.shape[0]
    kc = lax.broadcasted_iota(jnp.int32, (t, t), 0) // CHUNK
    qc = lax.broadcasted_iota(jnp.int32, (t, t), 1) // CHUNK
    return jnp.where(kc <= qc, s, -jnp.inf)


def _softmax_update(state, s, vt, smax=None):
    m_prev, l_prev, acc = state
    if smax is None:
        smax = jnp.max(s, axis=0, keepdims=True)
    m_new = jnp.maximum(m_prev, smax)
    alpha = jnp.exp2(m_prev - m_new)
    p = jnp.exp2(s - m_new)
    l_new = alpha * l_prev + jnp.sum(p, axis=0, keepdims=True)
    acc_new = alpha * acc + _dot(vt, p.astype(BF16))
    return m_new, l_new, acc_new


def _flash_kernel(q_ref, k_ref, vt_ref, beta_ref, qn_ref, kf_ref, o_ref, m_sc, l_sc, acc_sc, s_sc,
                  mx_sc):
    i = pl.program_id(2)
    t = ATT_T
    m_sc[...] = jnp.full(m_sc.shape, NEG_BIG, F32)
    l_sc[...] = jnp.zeros(l_sc.shape, F32)
    acc_sc[...] = jnp.zeros(acc_sc.shape, F32)

    def keys(j):
        return k_ref[pl.ds(pl.multiple_of(j * t, t), t), :]

    def load(half):
        return m_sc[half], l_sc[half], acc_sc[half]

    def store(half, state):
        m_sc[half], l_sc[half], acc_sc[half] = state

    def put(tile, scores):
        s_sc[tile] = scores
        mx_sc[tile] = jnp.max(scores, axis=0, keepdims=True)

    q0 = q_ref[:, 0:t]
    q1 = q_ref[:, t:2 * t]

    first_step = (pl.program_id(0) == 0) & (pl.program_id(1) == 0) & (i == 0)

    @pl.when(first_step)
    def _():
        kn0, kn1 = keys(0), keys(1)
        put(0, _scores(q0, kn0))
        put(1, _scores(q1, kn0))
        put(2, _scores(q0, kn1))
        put(3, _scores(q1, kn1))

    def body(jj, carry):
        v0, v1 = vt_ref[2 * jj], vt_ref[2 * jj + 1]
        kn0, kn1 = keys(2 * jj + 2), keys(2 * jj + 3)
        sn0 = _scores(q0, kn0)
        sn1 = _scores(q1, kn0)
        st0 = _softmax_update(load(0), s_sc[0], v0, mx_sc[0])
        put(0, sn0)
        st1 = _softmax_update(load(1), s_sc[1], v0, mx_sc[1])
        put(1, sn1)
        sn2 = _scores(q0, kn1)
        store(0, _softmax_update(st0, s_sc[2], v1, mx_sc[2]))
        put(2, sn2)
        store(1, _softmax_update(st1, s_sc[3], v1, mx_sc[3]))
        put(3, _scores(q1, kn1))
        return carry

    def body2(u, carry):
        body(2 * u, carry)
        return body(2 * u + 1, carry)

    lax.fori_loop(0, i // 2, body2, 0)

    @pl.when(i % 2 == 1)
    def _():
        body(i - 1, 0)

    v0, v1 = vt_ref[2 * i], vt_ref[2 * i + 1]
    qn0, qn1 = qn_ref[:, 0:t], qn_ref[:, t:2 * t]
    kf0, kf1 = kf_ref[0:t, :], kf_ref[t:2 * t, :]
    sn0 = _scores(qn0, kf0)
    sn1 = _scores(qn1, kf0)
    st0 = _softmax_update(load(0), _chunk_causal(s_sc[0]), v0)
    put(0, sn0)
    st1 = _softmax_update(load(1), s_sc[1], v0, mx_sc[1])
    put(1, sn1)
    st1 = _softmax_update(st1, _chunk_causal(s_sc[3]), v1)
    sn2 = _scores(qn0, kf1)
    sn3 = _scores(qn1, kf1)
    for half, (_, l_fin, acc_fin) in enumerate((st0, st1)):
        out = jnp.transpose(acc_fin / l_fin) * beta_ref[...]
        o_ref[half * t:(half + 1) * t, :] = out.astype(BF16)
    put(2, sn2)
    put(3, sn3)


def _flash_call(qt_pad, k_pad, vt, beta_mla, batch, seq):
    T = k_pad.shape[0]
    t = ATT_T
    tq = 2 * t
    nq = seq // tq
    nkv = seq // t
    n_steps = batch * MLA_HEADS * nq

    def successor(b, h, i):
        g = jnp.minimum((b * MLA_HEADS + h) * nq + i + 1, n_steps - 1)
        return g // (nq * MLA_HEADS), (g // nq) % MLA_HEADS, g % nq

    def next_q(b, h, i):
        b2, h2, i2 = successor(b, h, i)
        return h2, b2 * nq + i2

    def next_first_keys(b, h, i):
        b2, h2, _ = successor(b, h, i)
        return b2 * (seq // tq), h2

    return pl.pallas_call(
        _flash_kernel,
        grid=(batch, MLA_HEADS, nq),
        in_specs=[
            pl.BlockSpec((HEAD_PAD, tq), lambda b, h, i: (h, b * nq + i)),
            pl.BlockSpec((seq, HEAD_PAD), lambda b, h, i: (b, h)),
            pl.BlockSpec((nkv, MLA_V, t), lambda b, h, i: (b, h, 0)),
            pl.BlockSpec((1, MLA_V), lambda b, h, i: (0, h)),
            pl.BlockSpec((HEAD_PAD, tq), next_q),
            pl.BlockSpec((tq, HEAD_PAD), next_first_keys),
        ],
        out_specs=pl.BlockSpec((tq, MLA_V), lambda b, h, i: (b * nq + i, h)),
        out_shape=jax.ShapeDtypeStruct((T, MLA_OUT), BF16),
        scratch_shapes=[pltpu.VMEM((2, 1, t), F32), pltpu.VMEM((2, 1, t), F32),
                        pltpu.VMEM((2, MLA_V, t), F32), pltpu.VMEM((4, t, t), F32),
                        pltpu.VMEM((4, 1, t), F32)],
        compiler_params=pltpu.CompilerParams(
            dimension_semantics=("arbitrary", "arbitrary", "arbitrary"),
            vmem_limit_bytes=V7X_VMEM_LIMIT_BYTES),
        name="mla_flash",
    )(qt_pad, k_pad, vt, beta_mla, qt_pad, k_pad)


def _lane_cumsum(x):
    n = x.shape[-1]
    lane = lax.broadcasted_iota(jnp.int32, x.shape, x.ndim - 1)
    d = 1
    while d < n:
        x = x + jnp.where(lane >= d, pltpu.roll(x, d, x.ndim - 1), 0.0)
        d *= 2
    return x


def _mlstm_kernel(q_ref, k_ref, v_ref, o_ref, gt_ref, hg_ref, beta_ref, out_ref,
                  c_sc, n_sc, m_sc):
    L = ML_L

    @pl.when(pl.program_id(1) == 0)
    def _():
        c_sc[...] = jnp.zeros(c_sc.shape, F32)
        n_sc[...] = jnp.zeros(n_sc.shape, F32)
        m_sc[...] = jnp.full(m_sc.shape, -jnp.inf, F32)

    g8 = gt_ref[...]
    capped = GATE_CAP * jnp.tanh(g8 / GATE_CAP)
    logf = jnp.minimum(capped, 0.0) - jnp.log1p(jnp.exp(-jnp.abs(capped)))
    row8 = lax.broadcasted_iota(jnp.int32, g8.shape, 0)
    bsum = _lane_cumsum(jnp.where(row8 >= ML_HEADS, logf, 0.0))
    gb8 = jnp.where(row8 < ML_HEADS, capped - pltpu.roll(bsum, ML_HEADS, 0), bsum)

    rr = lax.broadcasted_iota(jnp.int32, (L, L), 0)
    cc = lax.broadcasted_iota(jnp.int32, (L, L), 1)
    causal = cc <= rr
    eye = cc == rr

    heads = range(ML_HEADS)
    qs = [q_ref[:, h * ML_QK:(h + 1) * ML_QK] for h in heads]
    ks = [k_ref[:, h * ML_QK:(h + 1) * ML_QK] for h in heads]
    vs = [v_ref[:, h * ML_V:(h + 1) * ML_V] for h in heads]
    cts = [c_sc[h] for h in heads]
    qk = [_nt_dot(qs[h], ks[h]) for h in heads]
    qc = [_dot(qs[h], cts[h].astype(BF16)) for h in heads]

    pending = []
    for h in heads:
        g_row = gb8[h:h + 1, :]
        b_row = gb8[ML_HEADS + h:ML_HEADS + h + 1, :]
        G = jnp.broadcast_to(g_row, (L, L))
        Bm = jnp.broadcast_to(b_row, (L, L))
        m_prev = m_sc[h:h + 1, 0:1]
        cummax = jnp.max(jnp.where(causal, G, -jnp.inf), axis=1, keepdims=True)
        mcol = jnp.maximum(m_prev, cummax)
        g_col = jnp.sum(jnp.where(eye, G, 0.0), axis=1, keepdims=True)
        b_col = jnp.sum(jnp.where(eye, Bm, 0.0), axis=1, keepdims=True)
        dmat = jnp.where(causal, jnp.exp(G - mcol), 0.0)
        inter_w = jnp.exp(m_prev - mcol)

        qh, kh, vh, ct = qs[h], ks[h], vs[h], cts[h]
        nrow = n_sc[h:h + 1, :]

        s = qk[h] * dmat
        num = _dot(s.astype(BF16), vh) + inter_w * qc[h]
        den = (jnp.sum(s, axis=1, keepdims=True)
               + inter_w * jnp.sum(qh.astype(F32) * nrow, axis=1, keepdims=True))
        hval = num / jnp.maximum(jnp.abs(den), jnp.exp(-(b_col + mcol)))

        m_last = mcol[L - 1:L, :]
        decay = inter_w[L - 1:L, :]
        w_col = jnp.exp(g_col - m_last)
        kw = kh.astype(F32) * w_col
        pending.append((kw.astype(BF16), vh, decay, ct))
        n_sc[h:h + 1, :] = decay * nrow + jnp.sum(kw, axis=0, keepdims=True)
        m_sc[h:h + 1, :] = jnp.broadcast_to(b_row[:, L - 1:L] + m_last, (1, LANES))

        hn = hval * lax.rsqrt(jnp.mean(hval * hval, axis=-1, keepdims=True) + RMS_EPS)
        hn = hn * hg_ref[:, h * ML_V:(h + 1) * ML_V]
        og = jax.nn.sigmoid(o_ref[:, h * ML_V:(h + 1) * ML_V].astype(F32))
        out_ref[:, h * ML_V:(h + 1) * ML_V] = (
            og * hn * beta_ref[:, h * ML_V:(h + 1) * ML_V]).astype(BF16)

    for h, (kwb, vh, decay, ct) in enumerate(pending):
        upd = lax.dot_general(kwb, vh, (((0,), (0,)), ((), ())),
                              preferred_element_type=F32)
        c_sc[h] = decay * ct + upd


def _mlstm_call(mq, mk, mv, mo, gates_t, head_g, beta_ml, batch, seq):
    T = mq.shape[0]
    L = ML_L
    nc = seq // L
    blk = lambda w: pl.BlockSpec((L, w), lambda b, c: (b * nc + c, 0))
    return pl.pallas_call(
        _mlstm_kernel,
        grid=(batch, nc),
        in_specs=[
            blk(ML_HEADS * ML_QK), blk(ML_HEADS * ML_QK), blk(ML_OUT), blk(ML_OUT),
            pl.BlockSpec((8, L), lambda b, c: (0, b * nc + c)),
            _const_spec((1, ML_OUT)),
            _const_spec((1, ML_OUT)),
        ],
        out_specs=blk(ML_OUT),
        out_shape=jax.ShapeDtypeStruct((T, ML_OUT), BF16),
        scratch_shapes=[pltpu.VMEM((ML_HEADS, ML_QK, ML_V), F32),
                        pltpu.VMEM((8, ML_QK), F32),
                        pltpu.VMEM((8, LANES), F32)],
        compiler_params=pltpu.CompilerParams(
            dimension_semantics=("arbitrary", "arbitrary"),
            vmem_limit_bytes=V7X_VMEM_LIMIT_BYTES),
        name="mlstm_scan",
    )(mq, mk, mv, mo, gates_t, head_g, beta_ml)


def _layernorm(y, g, b):
    mu = jnp.mean(y, axis=-1, keepdims=True)
    yc = y - mu
    return yc * lax.rsqrt(jnp.mean(yc * yc, axis=-1, keepdims=True) + LN_EPS) * g + b


def _outproj_kernel(a_ref, m_ref, x_ref, w_ref, g_ref, b_ref, o_ref):
    tm = o_ref.shape[0]
    sub = tm // OUT_SUBTILES

    def rows(r):
        return slice(r * sub, (r + 1) * sub)

    def project(r):
        return _dot(a_ref[rows(r), :], w_ref[0:MLA_OUT, :]) + _dot(m_ref[rows(r), :], w_ref[MLA_OUT:, :])

    def finish(r, acc):
        o_ref[rows(r), :] = _layernorm(DN_ALPHA * x_ref[rows(r), :] + acc, g_ref[...], b_ref[...])

    acc = project(0)
    for r in range(1, OUT_SUBTILES):
        nxt = project(r)
        finish(r - 1, acc)
        acc = nxt
    finish(OUT_SUBTILES - 1, acc)


def _outproj_call(attn, mem, x2, w_out, g, b):
    T = x2.shape[0]
    tm = OUT_TM
    return pl.pallas_call(
        _outproj_kernel,
        grid=(T // tm,),
        in_specs=[
            pl.BlockSpec((tm, MLA_OUT), lambda i: (i, 0)),
            pl.BlockSpec((tm, ML_OUT), lambda i: (i, 0)),
            pl.BlockSpec((tm, D_MODEL), lambda i: (i, 0)),
            _const_spec((MLA_OUT + ML_OUT, D_MODEL)),
            _const_spec((1, D_MODEL)),
            _const_spec((1, D_MODEL)),
        ],
        out_specs=pl.BlockSpec((tm, D_MODEL), lambda i: (i, 0)),
        out_shape=jax.ShapeDtypeStruct((T, D_MODEL), F32),
        compiler_params=pltpu.CompilerParams(
            dimension_semantics=("arbitrary",), vmem_limit_bytes=V7X_VMEM_LIMIT_BYTES),
        name="outproj_ln",
    )(attn, mem, x2, w_out, g, b)


def _gelu_tanh(x):
    c = math.sqrt(2.0 / math.pi)
    return 0.5 * x * (1.0 + jnp.tanh(c * (x + 0.044715 * (x * x * x))))


def _ffn_kernel(tiles_per_seq, h_ref, wg_ref, wv_ref, wd_ref, cw_ref, cb_ref, g_ref, b_ref,
                o_ref, hb_ref, gbuf_ref, val_ref, halo_ref):
    i = pl.program_id(0)
    f = pl.program_id(1)
    nf = pl.num_programs(1) - 1
    tm = FFN_TM
    cols = [slice(c * FFN_SUB, (c + 1) * FFN_SUB) for c in range(FFN_TF // FFN_SUB)]
    first = (i % tiles_per_seq) == 0

    def up_project(c):
        hb = hb_ref[...]
        return _dot(hb, wg_ref[:, c]), _dot(hb, wv_ref[:, c])

    def stash(c, gate, val):
        gbuf_ref[0:8, c] = jnp.where(first, 0.0, halo_ref[f, :, c])
        gbuf_ref[8:8 + tm, c] = gate
        halo_ref[f, :, c] = gate[tm - 8:tm, :]
        val_ref[:, c] = val

    def down_project(c, r0=0, nr=tm):
        conv = (cw_ref[0:1, c] * gbuf_ref[6 + r0:6 + r0 + nr, c]
                + cw_ref[1:2, c] * gbuf_ref[7 + r0:7 + r0 + nr, c]
                + cw_ref[2:3, c] * gbuf_ref[8 + r0:8 + r0 + nr, c] + cb_ref[:, c])
        act = (_gelu_tanh(conv) * val_ref[r0:r0 + nr, c]).astype(BF16)
        return _dot(act, wd_ref[c, :])

    def consume(r0=0, nr=tm):
        acc = None
        for c in cols:
            part = down_project(c, r0, nr)
            acc = part if acc is None else acc + part
        return o_ref[r0:r0 + nr, :] + acc

    @pl.when(jnp.logical_and(i == 0, f == 0))
    def _():
        halo_ref[...] = jnp.zeros(halo_ref.shape, F32)

    @pl.when(f == 0)
    def _():
        hb_ref[...] = h_ref[...].astype(BF16)
        o_ref[...] = DN_ALPHA * h_ref[...]
        for c in cols:
            stash(c, *up_project(c))

    @pl.when(jnp.logical_and(f > 0, f < nf))
    def _():
        ups = [up_project(c) for c in cols]
        o_ref[...] = consume()
        for c, (gate, val) in zip(cols, ups):
            stash(c, gate, val)

    @pl.when(f == nf)
    def _():
        nr = tm // FFN_LAST_SUBTILES
        for r in range(FFN_LAST_SUBTILES):
            rows = slice(r * nr, (r + 1) * nr)
            o_ref[rows, :] = _layernorm(consume(r * nr, nr), g_ref[...], b_ref[...])


def _ffn_call(h1, wg, wv, wd, conv_w, conv_b, g, b, seq):
    T = h1.shape[0]
    tm, tf = FFN_TM, FFN_TF
    nf = D_FF // tf
    kern = functools.partial(_ffn_kernel, seq // tm)
    produced = lambda i, f: (0, jnp.minimum(f, nf - 1))
    consumed = lambda i, f: (0, jnp.maximum(f - 1, 0))
    return pl.pallas_call(
        kern,
        grid=(T // tm, nf + 1),
        in_specs=[
            pl.BlockSpec((tm, D_MODEL), lambda i, f: (i, 0)),
            pl.BlockSpec((D_MODEL, tf), produced),
            pl.BlockSpec((D_MODEL, tf), produced),
            pl.BlockSpec((tf, D_MODEL), lambda i, f: (jnp.maximum(f - 1, 0), 0)),
            pl.BlockSpec((3, tf), consumed),
            pl.BlockSpec((1, tf), consumed),
            _const_spec((1, D_MODEL)),
            _const_spec((1, D_MODEL)),
        ],
        out_specs=pl.BlockSpec((tm, D_MODEL), lambda i, f: (i, 0)),
        out_shape=jax.ShapeDtypeStruct((T, D_MODEL), F32),
        scratch_shapes=[pltpu.VMEM((tm, D_MODEL), BF16),
                        pltpu.VMEM((tm + 8, tf), F32),
                        pltpu.VMEM((tm, tf), F32),
                        pltpu.VMEM((nf, 8, tf), F32)],
        compiler_params=pltpu.CompilerParams(
            dimension_semantics=("arbitrary", "arbitrary"),
            vmem_limit_bytes=V7X_VMEM_LIMIT_BYTES),
        name="convglu_ffn_ln",
    )(h1, wg, wv, wd, conv_w, conv_b, g, b)


def _prep_proj_weights(w_in, w_uq, b_igate, b_fgate):
    half = MLA_ROPE // 2
    o = 0
    w_cq = w_in[:, o:o + Q_LORA]; o += Q_LORA
    w_ckv = w_in[:, o:o + KV_LORA]; o += KV_LORA
    w_kr = w_in[:, o:o + MLA_ROPE]; o += MLA_ROPE
    w_mq = w_in[:, o:o + ML_HEADS * ML_QK]; o += ML_HEADS * ML_QK
    w_mk = w_in[:, o:o + ML_HEADS * ML_QK]; o += ML_HEADS * ML_QK
    w_mv = w_in[:, o:o + ML_OUT]; o += ML_OUT
    w_mo = w_in[:, o:o + ML_OUT]; o += ML_OUT
    w_gates = w_in[:, o:o + 2 * ML_HEADS]
    w_krot = jnp.concatenate([-w_kr[:, half:], w_kr[:, :half]], axis=1)
    z64 = jnp.zeros((D_MODEL, LANES - MLA_ROPE), w_in.dtype)
    w_all = jnp.concatenate([w_cq, w_ckv, w_kr, z64, w_krot, z64, w_mq, w_mk, w_mv, w_mo], axis=1)
    wg_t = w_gates.T
    gbias = jnp.concatenate([b_igate, b_fgate])[:, None]

    uq = w_uq.reshape(Q_LORA, MLA_HEADS, MLA_NOPE + MLA_ROPE)
    uq_nope = uq[:, :, :MLA_NOPE].reshape(Q_LORA, MLA_HEADS * MLA_NOPE)
    uq_r = uq[:, :, MLA_NOPE:]
    uq_rot = jnp.concatenate([-uq_r[:, :, half:], uq_r[:, :, :half]], axis=2)
    zpad = jnp.zeros((Q_LORA, MLA_HEADS, LANES - MLA_ROPE), w_uq.dtype)
    uq_r = jnp.concatenate([uq_r, zpad], axis=2).reshape(Q_LORA, MLA_HEADS * LANES)
    uq_rot = jnp.concatenate([uq_rot, zpad], axis=2).reshape(Q_LORA, MLA_HEADS * LANES)
    wuq_all = jnp.concatenate([uq_nope, uq_r, uq_rot], axis=1).T
    return w_all.astype(BF16), wg_t.astype(BF16), gbias.astype(F32), wuq_all.astype(BF16)


def kernel(x, positions, w_in, q_norm_g, kv_norm_g, w_uq, w_uk, w_uv, b_igate, b_fgate,
           ml_head_g, beta_mla, beta_ml, w_out, ln1_g, ln1_b, w_ffn_gate, w_ffn_val,
           conv_w, conv_b, w_down, ln2_g, ln2_b):
    B, S, D = x.shape
    T = B * S
    assert D == D_MODEL and w_in.shape[0] == DEPTH == 1
    assert S % (2 * ATT_T) == 0 and S % ML_L == 0 and S % FFN_TM == 0 and T % PROJ_TM == 0
    l = 0
    x2 = x.reshape(T, D)
    pos_col = positions.reshape(T, 1).astype(F32)
    inv_freq = 1.0 / (ROPE_BASE ** (jnp.arange(0, MLA_ROPE, 2, dtype=F32) / MLA_ROPE))
    invf = jnp.tile(inv_freq, LANES // (MLA_ROPE // 2))[None, :]

    w_all, wg_t, gbias, wuq_all = _prep_proj_weights(w_in[l], w_uq[l], b_igate[l], b_fgate[l])
    qt_pad, k_pad, vt, mq, mk, mv, mo, gates_t = _proj_call(
        x2, pos_col, invf, w_all, wg_t, gbias,
        q_norm_g[l][None, :], kv_norm_g[l][None, :], wuq_all,
        w_uk[l].astype(BF16), w_uv[l].T.astype(BF16))

    attn = _flash_call(qt_pad, k_pad, vt, beta_mla[l][None, :], B, S)
    mem = _mlstm_call(mq, mk, mv, mo, gates_t, ml_head_g[l][None, :], beta_ml[l][None, :], B, S)
    h1 = _outproj_call(attn, mem, x2, w_out[l].astype(BF16), ln1_g[l][None, :], ln1_b[l][None, :])
    out = _ffn_call(h1, w_ffn_gate[l].astype(BF16), w_ffn_val[l].astype(BF16),
                    w_down[l].astype(BF16), conv_w[l], conv_b[l][None, :],
                    ln2_g[l][None, :], ln2_b[l][None, :], S)
    return out.reshape(B, S, D)
```

```python
import functools
import math

import jax
import jax.numpy as jnp
from jax import lax
from jax.experimental import pallas as pl
from jax.experimental.pallas import tpu as pltpu

D_MODEL = 2048
CHUNK = 64
MLA_HEADS = 8
MLA_NOPE = 128
MLA_ROPE = 64
MLA_V = 128
Q_LORA = 512
KV_LORA = 256
ROPE_BASE = 10000.0
MLA_OUT = MLA_HEADS * MLA_V
ML_HEADS = 4
ML_QK = 128
ML_V = 256
ML_OUT = ML_HEADS * ML_V
GATE_CAP = 15.0
D_FF = 5632
RMS_EPS = 1e-6
LN_EPS = 1e-5
DEPTH = 1
DN_ALPHA = (2 * DEPTH) ** 0.25

LANES = 128
V7X_VMEM_LIMIT_BYTES = 58 * 1024 * 1024

PROJ_TM = 256
ATT_T = 512
ML_L = 256
OUT_TM = 512
OUT_SUBTILES = 4
FFN_TM = 512
FFN_TF = 512
FFN_SUB = 256
FFN_LAST_SUBTILES = 2

HEAD_PAD = 2 * LANES
PROJ_W = 4096
_C_CQ, _C_CKV, _C_KR, _C_KROT, _C_MQ, _C_MK, _C_MV, _C_MO = 0, 512, 768, 896, 1024, 1536, 2048, 3072

F32 = jnp.float32
BF16 = jnp.bfloat16
NEG_BIG = -1e30


def _nt_dot(a, b):
    return lax.dot_general(a, b, (((1,), (1,)), ((), ())), preferred_element_type=F32)


def _dot(a, b):
    return jnp.dot(a, b, preferred_element_type=F32)


def _const_spec(shape):
    nd = len(shape)
    return pl.BlockSpec(shape, lambda *_: (0,) * nd, pipeline_mode=pl.Buffered(1))


def _lane_cumsum(x):
    n = x.shape[-1]
    lane = lax.broadcasted_iota(jnp.int32, x.shape, x.ndim - 1)
    d = 1
    while d < n:
        x = x + jnp.where(lane >= d, pltpu.roll(x, d, x.ndim - 1), 0.0)
        d *= 2
    return x


def _proj_mlstm_kernel(chunks_per_seq,
                       x_ref, pos_ref, invf_ref, w_ref, wg_ref, gb_ref, qg_ref, kvg_ref,
                       wuq_ref, wuk_ref, wuv_ref, hg_ref, beta_ref,
                       q_ref, k_ref, vt_ref, mem_ref,
                       xb_ref, mq_s, mk_s, mv_s, mo_s, gt_s, c_sc, n_sc, m_sc):
    g = pl.program_id(0)
    n = pl.num_programs(0) - 1
    L = ML_L
    heads = range(ML_HEADS)
    pj = {}
    ml = {}

    def proj_latents():
        xb_ref[...] = x_ref[...].astype(BF16)
        xb = xb_ref[...]
        pj["cq"] = _dot(xb, w_ref[:, _C_CQ:_C_CQ + Q_LORA])
        pj["p1"] = _dot(xb, w_ref[:, _C_CKV:_C_MQ])
        pj["mq"] = _dot(xb, w_ref[:, _C_MQ:_C_MK]).astype(BF16)
        pj["mk"] = (_dot(xb, w_ref[:, _C_MK:_C_MV]) * (ML_QK ** -0.5)).astype(BF16)

    def proj_norms():
        ang = pos_ref[...] * invf_ref[...]
        pj["cos"] = jnp.cos(ang)
        pj["sin"] = jnp.sin(ang)
        cq, p1 = pj["cq"], pj["p1"]
        cq = cq * lax.rsqrt(jnp.mean(cq * cq, axis=-1, keepdims=True) + RMS_EPS) * qg_ref[...]
        pj["cqb"] = cq.astype(BF16)
        ckv = p1[:, 0:KV_LORA]
        ckv = ckv * lax.rsqrt(jnp.mean(ckv * ckv, axis=-1, keepdims=True) + RMS_EPS) * kvg_ref[...]
        pj["ckvb"] = ckv.astype(BF16)
        pj["k_rope"] = (p1[:, 256:384] * pj["cos"] + p1[:, 384:512] * pj["sin"]).astype(BF16)

    def proj_up():
        qscale = (MLA_NOPE + MLA_ROPE) ** -0.5 * math.log2(math.e)
        cqb, ckvb = pj["cqb"], pj["ckvb"]
        pj["qt_nope"] = _nt_dot(wuq_ref[0:1024, :], cqb) * qscale
        pj["qt_r"] = _nt_dot(wuq_ref[1024:2048, :], cqb)
        pj["qt_rot"] = _nt_dot(wuq_ref[2048:3072, :], cqb)
        pj["k_nope"] = _dot(ckvb, wuk_ref[...]).astype(BF16)
        vt_ref[0] = _nt_dot(wuv_ref[...], ckvb).astype(BF16)
        pj["qscale"] = qscale

    def proj_mlstm_inputs():
        xb = xb_ref[...]
        pj["mv"] = _dot(xb, w_ref[:, _C_MV:_C_MO]).astype(BF16)
        pj["mo"] = _dot(xb, w_ref[:, _C_MO:PROJ_W]).astype(BF16)
        pj["gt"] = _nt_dot(wg_ref[...], xb) + gb_ref[...]

    def proj_write():
        cos_t = jnp.transpose(pj["cos"])
        sin_t = jnp.transpose(pj["sin"])
        for h in range(MLA_HEADS):
            rows = slice(h * LANES, (h + 1) * LANES)
            q_ref[h * HEAD_PAD:h * HEAD_PAD + LANES, :] = pj["qt_nope"][rows, :].astype(BF16)
            q_ref[h * HEAD_PAD + LANES:(h + 1) * HEAD_PAD, :] = (
                (pj["qt_r"][rows, :] * cos_t + pj["qt_rot"][rows, :] * sin_t) * pj["qscale"]).astype(BF16)
            k_ref[:, h * HEAD_PAD:h * HEAD_PAD + LANES] = pj["k_nope"][:, h * LANES:(h + 1) * LANES]
            k_ref[:, h * HEAD_PAD + LANES:(h + 1) * HEAD_PAD] = pj["k_rope"]
        mq_s[...] = pj["mq"]
        mk_s[...] = pj["mk"]
        mv_s[...] = pj["mv"]
        mo_s[...] = pj["mo"]
        gt_s[...] = pj["gt"]

    def scan_scores():
        ml["qs"] = [mq_s[:, h * ML_QK:(h + 1) * ML_QK] for h in heads]
        ml["ks"] = [mk_s[:, h * ML_QK:(h + 1) * ML_QK] for h in heads]
        ml["vs"] = [mv_s[:, h * ML_V:(h + 1) * ML_V] for h in heads]
        ml["cts"] = [c_sc[h] for h in heads]
        ml["qk"] = [_nt_dot(ml["qs"][h], ml["ks"][h]) for h in heads]
        ml["qc"] = [_dot(ml["qs"][h], ml["cts"][h].astype(BF16)) for h in heads]

    def scan_heads():
        g8 = gt_s[...]
        capped = GATE_CAP * jnp.tanh(g8 / GATE_CAP)
        logf = jnp.minimum(capped, 0.0) - jnp.log1p(jnp.exp(-jnp.abs(capped)))
        row8 = lax.broadcasted_iota(jnp.int32, g8.shape, 0)
        bsum = _lane_cumsum(jnp.where(row8 >= ML_HEADS, logf, 0.0))
        gb8 = jnp.where(row8 < ML_HEADS, capped - pltpu.roll(bsum, ML_HEADS, 0), bsum)
        rr = lax.broadcasted_iota(jnp.int32, (L, L), 0)
        cc = lax.broadcasted_iota(jnp.int32, (L, L), 1)
        causal = cc <= rr
        eye = cc == rr
        pending = []
        for h in heads:
            g_row = gb8[h:h + 1, :]
            b_row = gb8[ML_HEADS + h:ML_HEADS + h + 1, :]
            G = jnp.broadcast_to(g_row, (L, L))
            Bm = jnp.broadcast_to(b_row, (L, L))
            m_prev = m_sc[h:h + 1, 0:1]
            cummax = jnp.max(jnp.where(causal, G, -jnp.inf), axis=1, keepdims=True)
            mcol = jnp.maximum(m_prev, cummax)
            g_col = jnp.sum(jnp.where(eye, G, 0.0), axis=1, keepdims=True)
            b_col = jnp.sum(jnp.where(eye, Bm, 0.0), axis=1, keepdims=True)
            dmat = jnp.where(causal, jnp.exp(G - mcol), 0.0)
            inter_w = jnp.exp(m_prev - mcol)

            qh, kh, vh, ct = ml["qs"][h], ml["ks"][h], ml["vs"][h], ml["cts"][h]
            nrow = n_sc[h:h + 1, :]

            s = ml["qk"][h] * dmat
            num = _dot(s.astype(BF16), vh) + inter_w * ml["qc"][h]
            den = (jnp.sum(s, axis=1, keepdims=True)
                   + inter_w * jnp.sum(qh.astype(F32) * nrow, axis=1, keepdims=True))
            hval = num / jnp.maximum(jnp.abs(den), jnp.exp(-(b_col + mcol)))

            m_last = mcol[L - 1:L, :]
            decay = inter_w[L - 1:L, :]
            w_col = jnp.exp(g_col - m_last)
            kw = kh.astype(F32) * w_col
            pending.append((kw.astype(BF16), vh, decay, ct))
            n_sc[h:h + 1, :] = decay * nrow + jnp.sum(kw, axis=0, keepdims=True)
            m_sc[h:h + 1, :] = jnp.broadcast_to(b_row[:, L - 1:L] + m_last, (1, LANES))

            hn = hval * lax.rsqrt(jnp.mean(hval * hval, axis=-1, keepdims=True) + RMS_EPS)
            hn = hn * hg_ref[:, h * ML_V:(h + 1) * ML_V]
            og = jax.nn.sigmoid(mo_s[:, h * ML_V:(h + 1) * ML_V].astype(F32))
            mem_ref[:, h * ML_V:(h + 1) * ML_V] = (
                og * hn * beta_ref[:, h * ML_V:(h + 1) * ML_V]).astype(BF16)
        ml["pending"] = pending

    def scan_update():
        for h, (kwb, vh, decay, ct) in enumerate(ml["pending"]):
            upd = lax.dot_general(kwb, vh, (((0,), (0,)), ((), ())),
                                  preferred_element_type=F32)
            c_sc[h] = decay * ct + upd

    @pl.when(jnp.logical_and(g > 0, (g - 1) % chunks_per_seq == 0))
    def _():
        c_sc[...] = jnp.zeros(c_sc.shape, F32)
        n_sc[...] = jnp.zeros(n_sc.shape, F32)
        m_sc[...] = jnp.full(m_sc.shape, -jnp.inf, F32)

    @pl.when(g == 0)
    def _():
        proj_latents()
        proj_norms()
        proj_up()
        proj_mlstm_inputs()
        proj_write()

    @pl.when(jnp.logical_and(g > 0, g < n))
    def _():
        scan_scores()
        proj_latents()
        proj_norms()
        scan_heads()
        proj_up()
        scan_update()
        proj_mlstm_inputs()
        proj_write()

    @pl.when(g == n)
    def _():
        scan_scores()
        scan_heads()
        scan_update()


def _proj_mlstm_call(x2, pos_col, invf, w_all, wg_t, gbias, qg, kvg, wuq_all, wuk, wuv,
                     head_g, beta_ml, seq):
    T = x2.shape[0]
    tm = PROJ_TM
    assert tm == ML_L
    n = T // tm
    cur = lambda g: jnp.minimum(g, n - 1)
    row = lambda w: pl.BlockSpec((tm, w), lambda g: (cur(g), 0))
    out_shapes = (
        jax.ShapeDtypeStruct((MLA_HEADS * HEAD_PAD, T), BF16),
        jax.ShapeDtypeStruct((T, MLA_HEADS * HEAD_PAD), BF16),
        jax.ShapeDtypeStruct((T // ATT_T, MLA_OUT, ATT_T), BF16),
        jax.ShapeDtypeStruct((T, ML_OUT), BF16),
    )
    kern = functools.partial(_proj_mlstm_kernel, seq // ML_L)
    return pl.pallas_call(
        kern,
        grid=(n + 1,),
        in_specs=[
            row(D_MODEL),
            pl.BlockSpec((tm, 1), lambda g: (cur(g), 0)),
            _const_spec((1, LANES)),
            _const_spec((D_MODEL, PROJ_W)),
            _const_spec((8, D_MODEL)),
            _const_spec((8, 1)),
            _const_spec((1, Q_LORA)),
            _const_spec((1, KV_LORA)),
            _const_spec((3072, Q_LORA)),
            _const_spec((KV_LORA, MLA_HEADS * MLA_NOPE)),
            _const_spec((MLA_OUT, KV_LORA)),
            _const_spec((1, ML_OUT)),
            _const_spec((1, ML_OUT)),
        ],
        out_specs=(
            pl.BlockSpec((MLA_HEADS * HEAD_PAD, tm), lambda g: (0, cur(g))),
            row(MLA_HEADS * HEAD_PAD),
            pl.BlockSpec((1, MLA_OUT, tm),
                         lambda g: (cur(g) // (ATT_T // tm), 0, cur(g) % (ATT_T // tm))),
            pl.BlockSpec((tm, ML_OUT), lambda g: (jnp.maximum(g - 1, 0), 0)),
        ),
        out_shape=out_shapes,
        scratch_shapes=[pltpu.VMEM((tm, D_MODEL), BF16),
                        pltpu.VMEM((tm, ML_HEADS * ML_QK), BF16),
                        pltpu.VMEM((tm, ML_HEADS * ML_QK), BF16),
                        pltpu.VMEM((tm, ML_OUT), BF16),
                        pltpu.VMEM((tm, ML_OUT), BF16),
                        pltpu.VMEM((8, tm), F32),
                        pltpu.VMEM((ML_HEADS, ML_QK, ML_V), F32),
                        pltpu.VMEM((8, ML_QK), F32),
                        pltpu.VMEM((8, LANES), F32)],
        compiler_params=pltpu.CompilerParams(
            dimension_semantics=("arbitrary",), vmem_limit_bytes=V7X_VMEM_LIMIT_BYTES),
        name="proj_mlstm",
    )(x2, pos_col, invf, w_all, wg_t, gbias, qg, kvg, wuq_all, wuk, wuv, head_g, beta_ml)


def _scores(qt, k):
    return _dot(k, qt)


def _chunk_causal(s):
    t = s.shape[0]
    kc = lax.broadcasted_iota(jnp.int32, (t, t), 0) // CHUNK
    qc = lax.broadcasted_iota(jnp.int32, (t, t), 1) // CHUNK
    return jnp.where(kc <= qc, s, -jnp.inf)


def _softmax_update(state, s, vt, smax=None):
    m_prev, l_prev, acc = state
    if smax is None:
        smax = jnp.max(s, axis=0, keepdims=True)
    m_new = jnp.maximum(m_prev, smax)
    alpha = jnp.exp2(m_prev - m_new)
    p = jnp.exp2(s - m_new)
    l_new = alpha * l_prev + jnp.sum(p, axis=0, keepdims=True)
    acc_new = alpha * acc + _dot(vt, p.astype(BF16))
    return m_new, l_new, acc_new


def _flash_kernel(q_ref, k_ref, vt_ref, beta_ref, qn_ref, kf_ref, o_ref, m_sc, l_sc, acc_sc, s_sc,
                  mx_sc):
    i = pl.program_id(2)
    t = ATT_T
    m_sc[...] = jnp.full(m_sc.shape, NEG_BIG, F32)
    l_sc[...] = jnp.zeros(l_sc.shape, F32)
    acc_sc[...] = jnp.zeros(acc_sc.shape, F32)

    def keys(j):
        return k_ref[pl.ds(pl.multiple_of(j * t, t), t), :]

    def load(half):
        return m_sc[half], l_sc[half], acc_sc[half]

    def store(half, state):
        m_sc[half], l_sc[half], acc_sc[half] = state

    def put(tile, scores):
        s_sc[tile] = scores
        mx_sc[tile] = jnp.max(scores, axis=0, keepdims=True)

    q0 = q_ref[:, 0:t]
    q1 = q_ref[:, t:2 * t]

    first_step = (pl.program_id(0) == 0) & (pl.program_id(1) == 0) & (i == 0)

    @pl.when(first_step)
    def _():
        kn0, kn1 = keys(0), keys(1)
        put(0, _scores(q0, kn0))
        put(1, _scores(q1, kn0))
        put(2, _scores(q0, kn1))
        put(3, _scores(q1, kn1))

    def body(jj, carry):
        v0, v1 = vt_ref[2 * jj], vt_ref[2 * jj + 1]
        kn0, kn1 = keys(2 * jj + 2), keys(2 * jj + 3)
        sn0 = _scores(q0, kn0)
        sn1 = _scores(q1, kn0)
        st0 = _softmax_update(load(0), s_sc[0], v0, mx_sc[0])
        put(0, sn0)
        st1 = _softmax_update(load(1), s_sc[1], v0, mx_sc[1])
        put(1, sn1)
        sn2 = _scores(q0, kn1)
        store(0, _softmax_update(st0, s_sc[2], v1, mx_sc[2]))
        put(2, sn2)
        store(1, _softmax_update(st1, s_sc[3], v1, mx_sc[3]))
        put(3, _scores(q1, kn1))
        return carry

    def body2(u, carry):
        body(2 * u, carry)
        return body(2 * u + 1, carry)

    lax.fori_loop(0, i // 2, body2, 0)

    @pl.when(i % 2 == 1)
    def _():
        body(i - 1, 0)

    v0, v1 = vt_ref[2 * i], vt_ref[2 * i + 1]
    qn0, qn1 = qn_ref[:, 0:t], qn_ref[:, t:2 * t]
    kf0, kf1 = kf_ref[0:t, :], kf_ref[t:2 * t, :]
    sn0 = _scores(qn0, kf0)
    sn1 = _scores(qn1, kf0)
    st0 = _softmax_update(load(0), _chunk_causal(s_sc[0]), v0)
    put(0, sn0)
    st1 = _softmax_update(load(1), s_sc[1], v0, mx_sc[1])
    put(1, sn1)
    st1 = _softmax_update(st1, _chunk_causal(s_sc[3]), v1)
    sn2 = _scores(qn0, kf1)
    sn3 = _scores(qn1, kf1)
    for half, (_, l_fin, acc_fin) in enumerate((st0, st1)):
        out = jnp.transpose(acc_fin / l_fin) * beta_ref[...]
        o_ref[half * t:(half + 1) * t, :] = out.astype(BF16)
    put(2, sn2)
    put(3, sn3)


def _flash_call(qt_pad, k_pad, vt, beta_mla, batch, seq):
    T = k_pad.shape[0]
    t = ATT_T
    tq = 2 * t
    nq = seq // tq
    nkv = seq // t
    n_steps = batch * MLA_HEADS * nq

    def successor(b, h, i):
        g = jnp.minimum((b * MLA_HEADS + h) * nq + i + 1, n_steps - 1)
        return g // (nq * MLA_HEADS), (g // nq) % MLA_HEADS, g % nq

    def next_q(b, h, i):
        b2, h2, i2 = successor(b, h, i)
        return h2, b2 * nq + i2

    def next_first_keys(b, h, i):
        b2, h2, _ = successor(b, h, i)
        return b2 * (seq // tq), h2

    return pl.pallas_call(
        _flash_kernel,
        grid=(batch, MLA_HEADS, nq),
        in_specs=[
            pl.BlockSpec((HEAD_PAD, tq), lambda b, h, i: (h, b * nq + i)),
            pl.BlockSpec((seq, HEAD_PAD), lambda b, h, i: (b, h)),
            pl.BlockSpec((nkv, MLA_V, t), lambda b, h, i: (b, h, 0)),
            pl.BlockSpec((1, MLA_V), lambda b, h, i: (0, h)),
            pl.BlockSpec((HEAD_PAD, tq), next_q),
            pl.BlockSpec((tq, HEAD_PAD), next_first_keys),
        ],
        out_specs=pl.BlockSpec((tq, MLA_V), lambda b, h, i: (b * nq + i, h)),
        out_shape=jax.ShapeDtypeStruct((T, MLA_OUT), BF16),
        scratch_shapes=[pltpu.VMEM((2, 1, t), F32), pltpu.VMEM((2, 1, t), F32),
                        pltpu.VMEM((2, MLA_V, t), F32), pltpu.VMEM((4, t, t), F32),
                        pltpu.VMEM((4, 1, t), F32)],
        compiler_params=pltpu.CompilerParams(
            dimension_semantics=("arbitrary", "arbitrary", "arbitrary"),
            vmem_limit_bytes=V7X_VMEM_LIMIT_BYTES),
        name="mla_flash",
    )(qt_pad, k_pad, vt, beta_mla, qt_pad, k_pad)


def _layernorm(y, g, b):
    mu = jnp.mean(y, axis=-1, keepdims=True)
    yc = y - mu
    return yc * lax.rsqrt(jnp.mean(yc * yc, axis=-1, keepdims=True) + LN_EPS) * g + b


def _outproj_kernel(a_ref, m_ref, x_ref, w_ref, g_ref, b_ref, o_ref):
    tm = o_ref.shape[0]
    sub = tm // OUT_SUBTILES

    def rows(r):
        return slice(r * sub, (r + 1) * sub)

    def project(r):
        return _dot(a_ref[rows(r), :], w_ref[0:MLA_OUT, :]) + _dot(m_ref[rows(r), :], w_ref[MLA_OUT:, :])

    def finish(r, acc):
        o_ref[rows(r), :] = _layernorm(DN_ALPHA * x_ref[rows(r), :] + acc, g_ref[...], b_ref[...])

    acc = project(0)
    for r in range(1, OUT_SUBTILES):
        nxt = project(r)
        finish(r - 1, acc)
        acc = nxt
    finish(OUT_SUBTILES - 1, acc)


def _outproj_call(attn, mem, x2, w_out, g, b):
    T = x2.shape[0]
    tm = OUT_TM
    return pl.pallas_call(
        _outproj_kernel,
        grid=(T // tm,),
        in_specs=[
            pl.BlockSpec((tm, MLA_OUT), lambda i: (i, 0)),
            pl.BlockSpec((tm, ML_OUT), lambda i: (i, 0)),
            pl.BlockSpec((tm, D_MODEL), lambda i: (i, 0)),
            _const_spec((MLA_OUT + ML_OUT, D_MODEL)),
            _const_spec((1, D_MODEL)),
            _const_spec((1, D_MODEL)),
        ],
        out_specs=pl.BlockSpec((tm, D_MODEL), lambda i: (i, 0)),
        out_shape=jax.ShapeDtypeStruct((T, D_MODEL), F32),
        compiler_params=pltpu.CompilerParams(
            dimension_semantics=("arbitrary",), vmem_limit_bytes=V7X_VMEM_LIMIT_BYTES),
        name="outproj_ln",
    )(attn, mem, x2, w_out, g, b)


def _gelu_tanh(x):
    c = math.sqrt(2.0 / math.pi)
    return 0.5 * x * (1.0 + jnp.tanh(c * (x + 0.044715 * (x * x * x))))


def _ffn_kernel(tiles_per_seq, h_ref, wg_ref, wv_ref, wd_ref, cw_ref, cb_ref, g_ref, b_ref,
                o_ref, hb_ref, gbuf_ref, val_ref, halo_ref):
    i = pl.program_id(0)
    f = pl.program_id(1)
    nf = pl.num_programs(1) - 1
    tm = FFN_TM
    cols = [slice(c * FFN_SUB, (c + 1) * FFN_SUB) for c in range(FFN_TF // FFN_SUB)]
    first = (i % tiles_per_seq) == 0

    def up_project(c):
        hb = hb_ref[...]
        return _dot(hb, wg_ref[:, c]), _dot(hb, wv_ref[:, c])

    def stash(c, gate, val):
        gbuf_ref[0:8, c] = jnp.where(first, 0.0, halo_ref[f, :, c])
        gbuf_ref[8:8 + tm, c] = gate
        halo_ref[f, :, c] = gate[tm - 8:tm, :]
        val_ref[:, c] = val

    def down_project(c, r0=0, nr=tm):
        conv = (cw_ref[0:1, c] * gbuf_ref[6 + r0:6 + r0 + nr, c]
                + cw_ref[1:2, c] * gbuf_ref[7 + r0:7 + r0 + nr, c]
                + cw_ref[2:3, c] * gbuf_ref[8 + r0:8 + r0 + nr, c] + cb_ref[:, c])
        act = (_gelu_tanh(conv) * val_ref[r0:r0 + nr, c]).astype(BF16)
        return _dot(act, wd_ref[c, :])

    def consume(r0=0, nr=tm):
        acc = None
        for c in cols:
            part = down_project(c, r0, nr)
            acc = part if acc is None else acc + part
        return o_ref[r0:r0 + nr, :] + acc

    @pl.when(jnp.logical_and(i == 0, f == 0))
    def _():
        halo_ref[...] = jnp.zeros(halo_ref.shape, F32)

    @pl.when(f == 0)
    def _():
        hb_ref[...] = h_ref[...].astype(BF16)
        o_ref[...] = DN_ALPHA * h_ref[...]
        for c in cols:
            stash(c, *up_project(c))

    @pl.when(jnp.logical_and(f > 0, f < nf))
    def _():
        ups = [up_project(c) for c in cols]
        o_ref[...] = consume()
        for c, (gate, val) in zip(cols, ups):
            stash(c, gate, val)

    @pl.when(f == nf)
    def _():
        nr = tm // FFN_LAST_SUBTILES
        for r in range(FFN_LAST_SUBTILES):
            rows = slice(r * nr, (r + 1) * nr)
            o_ref[rows, :] = _layernorm(consume(r * nr, nr), g_ref[...], b_ref[...])


def _ffn_call(h1, wg, wv, wd, conv_w, conv_b, g, b, seq):
    T = h1.shape[0]
    tm, tf = FFN_TM, FFN_TF
    nf = D_FF // tf
    kern = functools.partial(_ffn_kernel, seq // tm)
    produced = lambda i, f: (0, jnp.minimum(f, nf - 1))
    consumed = lambda i, f: (0, jnp.maximum(f - 1, 0))
    return pl.pallas_call(
        kern,
        grid=(T // tm, nf + 1),
        in_specs=[
            pl.BlockSpec((tm, D_MODEL), lambda i, f: (i, 0)),
            pl.BlockSpec((D_MODEL, tf), produced),
            pl.BlockSpec((D_MODEL, tf), produced),
            pl.BlockSpec((tf, D_MODEL), lambda i, f: (jnp.maximum(f - 1, 0), 0)),
            pl.BlockSpec((3, tf), consumed),
            pl.BlockSpec((1, tf), consumed),
            _const_spec((1, D_MODEL)),
            _const_spec((1, D_MODEL)),
        ],
        out_specs=pl.BlockSpec((tm, D_MODEL), lambda i, f: (i, 0)),
        out_shape=jax.ShapeDtypeStruct((T, D_MODEL), F32),
        scratch_shapes=[pltpu.VMEM((tm, D_MODEL), BF16),
                        pltpu.VMEM((tm + 8, tf), F32),
                        pltpu.VMEM((tm, tf), F32),
                        pltpu.VMEM((nf, 8, tf), F32)],
        compiler_params=pltpu.CompilerParams(
            dimension_semantics=("arbitrary", "arbitrary"),
            vmem_limit_bytes=V7X_VMEM_LIMIT_BYTES),
        name="convglu_ffn_ln",
    )(h1, wg, wv, wd, conv_w, conv_b, g, b)


def _prep_proj_weights(w_in, w_uq, b_igate, b_fgate):
    half = MLA_ROPE // 2
    o = 0
    w_cq = w_in[:, o:o + Q_LORA]; o += Q_LORA
    w_ckv = w_in[:, o:o + KV_LORA]; o += KV_LORA
    w_kr = w_in[:, o:o + MLA_ROPE]; o += MLA_ROPE
    w_mq = w_in[:, o:o + ML_HEADS * ML_QK]; o += ML_HEADS * ML_QK
    w_mk = w_in[:, o:o + ML_HEADS * ML_QK]; o += ML_HEADS * ML_QK
    w_mv = w_in[:, o:o + ML_OUT]; o += ML_OUT
    w_mo = w_in[:, o:o + ML_OUT]; o += ML_OUT
    w_gates = w_in[:, o:o + 2 * ML_HEADS]
    w_krot = jnp.concatenate([-w_kr[:, half:], w_kr[:, :half]], axis=1)
    z64 = jnp.zeros((D_MODEL, LANES - MLA_ROPE), w_in.dtype)
    w_all = jnp.concatenate([w_cq, w_ckv, w_kr, z64, w_krot, z64, w_mq, w_mk, w_mv, w_mo], axis=1)
    wg_t = w_gates.T
    gbias = jnp.concatenate([b_igate, b_fgate])[:, None]

    uq = w_uq.reshape(Q_LORA, MLA_HEADS, MLA_NOPE + MLA_ROPE)
    uq_nope = uq[:, :, :MLA_NOPE].reshape(Q_LORA, MLA_HEADS * MLA_NOPE)
    uq_r = uq[:, :, MLA_NOPE:]
    uq_rot = jnp.concatenate([-uq_r[:, :, half:], uq_r[:, :, :half]], axis=2)
    zpad = jnp.zeros((Q_LORA, MLA_HEADS, LANES - MLA_ROPE), w_uq.dtype)
    uq_r = jnp.concatenate([uq_r, zpad], axis=2).reshape(Q_LORA, MLA_HEADS * LANES)
    uq_rot = jnp.concatenate([uq_rot, zpad], axis=2).reshape(Q_LORA, MLA_HEADS * LANES)
    wuq_all = jnp.concatenate([uq_nope, uq_r, uq_rot], axis=1).T
    return w_all.astype(BF16), wg_t.astype(BF16), gbias.astype(F32), wuq_all.astype(BF16)


def kernel(x, positions, w_in, q_norm_g, kv_norm_g, w_uq, w_uk, w_uv, b_igate, b_fgate,
           ml_head_g, beta_mla, beta_ml, w_out, ln1_g, ln1_b, w_ffn_gate, w_ffn_val,
           conv_w, conv_b, w_down, ln2_g, ln2_b):
    B, S, D = x.shape
    T = B * S
    assert D == D_MODEL and w_in.shape[0] == DEPTH == 1
    assert S % (2 * ATT_T) == 0 and S % ML_L == 0 and S % FFN_TM == 0 and T % PROJ_TM == 0
    l = 0
    x2 = x.reshape(T, D)
    pos_col = positions.reshape(T, 1).astype(F32)
    inv_freq = 1.0 / (ROPE_BASE ** (jnp.arange(0, MLA_ROPE, 2, dtype=F32) / MLA_ROPE))
    invf = jnp.tile(inv_freq, LANES // (MLA_ROPE // 2))[None, :]

    w_all, wg_t, gbias, wuq_all = _prep_proj_weights(w_in[l], w_uq[l], b_igate[l], b_fgate[l])
    qt_pad, k_pad, vt, mem = _proj_mlstm_call(
        x2, pos_col, invf, w_all, wg_t, gbias,
        q_norm_g[l][None, :], kv_norm_g[l][None, :], wuq_all,
        w_uk[l].astype(BF16), w_uv[l].T.astype(BF16),
        ml_head_g[l][None, :], beta_ml[l][None, :], S)

    attn = _flash_call(qt_pad, k_pad, vt, beta_mla[l][None, :], B, S)
    h1 = _outproj_call(attn, mem, x2, w_out[l].astype(BF16), ln1_g[l][None, :], ln1_b[l][None, :])
    out = _ffn_call(h1, w_ffn_gate[l].astype(BF16), w_ffn_val[l].astype(BF16),
                    w_down[l].astype(BF16), conv_w[l], conv_b[l][None, :],
                    ln2_g[l][None, :], ln2_b[l][None, :], S)
    return out.reshape(B, S, D)
```

```python
import functools
import math

import jax
import jax.numpy as jnp
from jax import lax
from jax.experimental import pallas as pl
from jax.experimental.pallas import tpu as pltpu

D_MODEL = 2048
CHUNK = 64
MLA_HEADS = 8
MLA_NOPE = 128
MLA_ROPE = 64
MLA_V = 128
Q_LORA = 512
KV_LORA = 256
ROPE_BASE = 10000.0
MLA_OUT = MLA_HEADS * MLA_V
ML_HEADS = 4
ML_QK = 128
ML_V = 256
ML_OUT = ML_HEADS * ML_V
GATE_CAP = 15.0
D_FF = 5632
RMS_EPS = 1e-6
LN_EPS = 1e-5
DEPTH = 1
DN_ALPHA = (2 * DEPTH) ** 0.25

LANES = 128
V7X_VMEM_LIMIT_BYTES = 58 * 1024 * 1024

PROJ_TM = 256
ATT_T = 512
ML_L = 256
OUT_TM = 1024
OUT_SUBTILES = 8
FFN_TM = 512
FFN_TF = 512
FFN_SUB = 256
FFN_LAST_SUBTILES = 2

HEAD_PAD = 2 * LANES
PROJ_W = 4096
_C_CQ, _C_CKV, _C_KR, _C_KROT, _C_MQ, _C_MK, _C_MV, _C_MO = 0, 512, 768, 896, 1024, 1536, 2048, 3072

F32 = jnp.float32
BF16 = jnp.bfloat16
NEG_BIG = -1e30


def _nt_dot(a, b):
    return lax.dot_general(a, b, (((1,), (1,)), ((), ())), preferred_element_type=F32)


def _dot(a, b):
    return jnp.dot(a, b, preferred_element_type=F32)


def _const_spec(shape):
    nd = len(shape)
    return pl.BlockSpec(shape, lambda *_: (0,) * nd, pipeline_mode=pl.Buffered(1))


def _lane_cumsum(x):
    n = x.shape[-1]
    lane = lax.broadcasted_iota(jnp.int32, x.shape, x.ndim - 1)
    d = 1
    while d < n:
        x = x + jnp.where(lane >= d, pltpu.roll(x, d, x.ndim - 1), 0.0)
        d *= 2
    return x


def _proj_mlstm_kernel(chunks_per_seq,
                       x_ref, pos_ref, invf_ref, w_ref, wg_ref, gb_ref, qg_ref, kvg_ref,
                       wuq_ref, wuk_ref, wuv_ref, hg_ref, beta_ref,
                       q_ref, k_ref, vt_ref, mem_ref,
                       xb_ref, mq_s, mk_s, mv_s, mo_s, gt_s, c_sc, n_sc, m_sc):
    g = pl.program_id(0)
    n = pl.num_programs(0) - 1
    L = ML_L
    heads = range(ML_HEADS)
    pj = {}
    ml = {}

    def proj_latents():
        xb_ref[...] = x_ref[...].astype(BF16)
        xb = xb_ref[...]
        pj["cq"] = _dot(xb, w_ref[:, _C_CQ:_C_CQ + Q_LORA])
        pj["p1"] = _dot(xb, w_ref[:, _C_CKV:_C_MQ])
        pj["mq"] = _dot(xb, w_ref[:, _C_MQ:_C_MK]).astype(BF16)
        pj["mk"] = (_dot(xb, w_ref[:, _C_MK:_C_MV]) * (ML_QK ** -0.5)).astype(BF16)

    def proj_norms():
        ang = pos_ref[...] * invf_ref[...]
        pj["cos"] = jnp.cos(ang)
        pj["sin"] = jnp.sin(ang)
        cq, p1 = pj["cq"], pj["p1"]
        cq = cq * lax.rsqrt(jnp.mean(cq * cq, axis=-1, keepdims=True) + RMS_EPS) * qg_ref[...]
        pj["cqb"] = cq.astype(BF16)
        ckv = p1[:, 0:KV_LORA]
        ckv = ckv * lax.rsqrt(jnp.mean(ckv * ckv, axis=-1, keepdims=True) + RMS_EPS) * kvg_ref[...]
        pj["ckvb"] = ckv.astype(BF16)
        pj["k_rope"] = (p1[:, 256:384] * pj["cos"] + p1[:, 384:512] * pj["sin"]).astype(BF16)

    def proj_up():
        qscale = (MLA_NOPE + MLA_ROPE) ** -0.5 * math.log2(math.e)
        cqb, ckvb = pj["cqb"], pj["ckvb"]
        pj["qt_nope"] = _nt_dot(wuq_ref[0:1024, :], cqb) * qscale
        pj["qt_r"] = _nt_dot(wuq_ref[1024:2048, :], cqb)
        pj["qt_rot"] = _nt_dot(wuq_ref[2048:3072, :], cqb)
        pj["k_nope"] = _dot(ckvb, wuk_ref[...]).astype(BF16)
        vt_ref[0] = _nt_dot(wuv_ref[...], ckvb).astype(BF16)
        pj["qscale"] = qscale

    def proj_mlstm_inputs():
        xb = xb_ref[...]
        pj["mv"] = _dot(xb, w_ref[:, _C_MV:_C_MO]).astype(BF16)
        pj["mo"] = _dot(xb, w_ref[:, _C_MO:PROJ_W]).astype(BF16)
        pj["gt"] = _nt_dot(wg_ref[...], xb) + gb_ref[...]

    def proj_write():
        cos_t = jnp.transpose(pj["cos"])
        sin_t = jnp.transpose(pj["sin"])
        for h in range(MLA_HEADS):
            rows = slice(h * LANES, (h + 1) * LANES)
            q_ref[h * HEAD_PAD:h * HEAD_PAD + LANES, :] = pj["qt_nope"][rows, :].astype(BF16)
            q_ref[h * HEAD_PAD + LANES:(h + 1) * HEAD_PAD, :] = (
                (pj["qt_r"][rows, :] * cos_t + pj["qt_rot"][rows, :] * sin_t) * pj["qscale"]).astype(BF16)
            k_ref[:, h * HEAD_PAD:h * HEAD_PAD + LANES] = pj["k_nope"][:, h * LANES:(h + 1) * LANES]
            k_ref[:, h * HEAD_PAD + LANES:(h + 1) * HEAD_PAD] = pj["k_rope"]
        mq_s[...] = pj["mq"]
        mk_s[...] = pj["mk"]
        mv_s[...] = pj["mv"]
        mo_s[...] = pj["mo"]
        gt_s[...] = pj["gt"]

    def scan_scores():
        ml["qs"] = [mq_s[:, h * ML_QK:(h + 1) * ML_QK] for h in heads]
        ml["ks"] = [mk_s[:, h * ML_QK:(h + 1) * ML_QK] for h in heads]
        ml["vs"] = [mv_s[:, h * ML_V:(h + 1) * ML_V] for h in heads]
        ml["cts"] = [c_sc[h] for h in heads]
        ml["qk"] = [_nt_dot(ml["qs"][h], ml["ks"][h]) for h in heads]
        ml["qc"] = [_dot(ml["qs"][h], ml["cts"][h].astype(BF16)) for h in heads]

    def scan_heads():
        g8 = gt_s[...]
        capped = GATE_CAP * jnp.tanh(g8 / GATE_CAP)
        logf = jnp.minimum(capped, 0.0) - jnp.log1p(jnp.exp(-jnp.abs(capped)))
        row8 = lax.broadcasted_iota(jnp.int32, g8.shape, 0)
        bsum = _lane_cumsum(jnp.where(row8 >= ML_HEADS, logf, 0.0))
        gb8 = jnp.where(row8 < ML_HEADS, capped - pltpu.roll(bsum, ML_HEADS, 0), bsum)
        rr = lax.broadcasted_iota(jnp.int32, (L, L), 0)
        cc = lax.broadcasted_iota(jnp.int32, (L, L), 1)
        causal = cc <= rr
        eye = cc == rr
        pending = []
        for h in heads:
            g_row = gb8[h:h + 1, :]
            b_row = gb8[ML_HEADS + h:ML_HEADS + h + 1, :]
            G = jnp.broadcast_to(g_row, (L, L))
            Bm = jnp.broadcast_to(b_row, (L, L))
            m_prev = m_sc[h:h + 1, 0:1]
            cummax = jnp.max(jnp.where(causal, G, -jnp.inf), axis=1, keepdims=True)
            mcol = jnp.maximum(m_prev, cummax)
            g_col = jnp.sum(jnp.where(eye, G, 0.0), axis=1, keepdims=True)
            b_col = jnp.sum(jnp.where(eye, Bm, 0.0), axis=1, keepdims=True)
            dmat = jnp.where(causal, jnp.exp(G - mcol), 0.0)
            inter_w = jnp.exp(m_prev - mcol)

            qh, kh, vh, ct = ml["qs"][h], ml["ks"][h], ml["vs"][h], ml["cts"][h]
            nrow = n_sc[h:h + 1, :]

            s = ml["qk"][h] * dmat
            num = _dot(s.astype(BF16), vh) + inter_w * ml["qc"][h]
            den = (jnp.sum(s, axis=1, keepdims=True)
                   + inter_w * jnp.sum(qh.astype(F32) * nrow, axis=1, keepdims=True))
            hval = num / jnp.maximum(jnp.abs(den), jnp.exp(-(b_col + mcol)))

            m_last = mcol[L - 1:L, :]
            decay = inter_w[L - 1:L, :]
            w_col = jnp.exp(g_col - m_last)
            kw = kh.astype(F32) * w_col
            pending.append((kw.astype(BF16), vh, decay, ct))
            n_sc[h:h + 1, :] = decay * nrow + jnp.sum(kw, axis=0, keepdims=True)
            m_sc[h:h + 1, :] = jnp.broadcast_to(b_row[:, L - 1:L] + m_last, (1, LANES))

            hn = hval * lax.rsqrt(jnp.mean(hval * hval, axis=-1, keepdims=True) + RMS_EPS)
            hn = hn * hg_ref[:, h * ML_V:(h + 1) * ML_V]
            og = jax.nn.sigmoid(mo_s[:, h * ML_V:(h + 1) * ML_V].astype(F32))
            mem_ref[:, h * ML_V:(h + 1) * ML_V] = (
                og * hn * beta_ref[:, h * ML_V:(h + 1) * ML_V]).astype(BF16)
        ml["pending"] = pending

    def scan_update():
        for h, (kwb, vh, decay, ct) in enumerate(ml["pending"]):
            upd = lax.dot_general(kwb, vh, (((0,), (0,)), ((), ())),
                                  preferred_element_type=F32)
            c_sc[h] = decay * ct + upd

    @pl.when(jnp.logical_and(g > 0, (g - 1) % chunks_per_seq == 0))
    def _():
        c_sc[...] = jnp.zeros(c_sc.shape, F32)
        n_sc[...] = jnp.zeros(n_sc.shape, F32)
        m_sc[...] = jnp.full(m_sc.shape, -jnp.inf, F32)

    @pl.when(g == 0)
    def _():
        proj_latents()
        proj_norms()
        proj_up()
        proj_mlstm_inputs()
        proj_write()

    @pl.when(jnp.logical_and(g > 0, g < n))
    def _():
        scan_scores()
        proj_latents()
        proj_norms()
        scan_heads()
        proj_up()
        scan_update()
        proj_mlstm_inputs()
        proj_write()

    @pl.when(g == n)
    def _():
        scan_scores()
        scan_heads()
        scan_update()


def _proj_mlstm_call(x2, pos_col, invf, w_all, wg_t, gbias, qg, kvg, wuq_all, wuk, wuv,
                     head_g, beta_ml, seq):
    T = x2.shape[0]
    tm = PROJ_TM
    assert tm == ML_L
    n = T // tm
    cur = lambda g: jnp.minimum(g, n - 1)
    row = lambda w: pl.BlockSpec((tm, w), lambda g: (cur(g), 0))
    out_shapes = (
        jax.ShapeDtypeStruct((MLA_HEADS * HEAD_PAD, T), BF16),
        jax.ShapeDtypeStruct((T, MLA_HEADS * HEAD_PAD), BF16),
        jax.ShapeDtypeStruct((T // ATT_T, MLA_OUT, ATT_T), BF16),
        jax.ShapeDtypeStruct((T, ML_OUT), BF16),
    )
    kern = functools.partial(_proj_mlstm_kernel, seq // ML_L)
    return pl.pallas_call(
        kern,
        grid=(n + 1,),
        in_specs=[
            row(D_MODEL),
            pl.BlockSpec((tm, 1), lambda g: (cur(g), 0)),
            _const_spec((1, LANES)),
            _const_spec((D_MODEL, PROJ_W)),
            _const_spec((8, D_MODEL)),
            _const_spec((8, 1)),
            _const_spec((1, Q_LORA)),
            _const_spec((1, KV_LORA)),
            _const_spec((3072, Q_LORA)),
            _const_spec((KV_LORA, MLA_HEADS * MLA_NOPE)),
            _const_spec((MLA_OUT, KV_LORA)),
            _const_spec((1, ML_OUT)),
            _const_spec((1, ML_OUT)),
        ],
        out_specs=(
            pl.BlockSpec((MLA_HEADS * HEAD_PAD, tm), lambda g: (0, cur(g))),
            row(MLA_HEADS * HEAD_PAD),
            pl.BlockSpec((1, MLA_OUT, tm),
                         lambda g: (cur(g) // (ATT_T // tm), 0, cur(g) % (ATT_T // tm))),
            pl.BlockSpec((tm, ML_OUT), lambda g: (jnp.maximum(g - 1, 0), 0)),
        ),
        out_shape=out_shapes,
        scratch_shapes=[pltpu.VMEM((tm, D_MODEL), BF16),
                        pltpu.VMEM((tm, ML_HEADS * ML_QK), BF16),
                        pltpu.VMEM((tm, ML_HEADS * ML_QK), BF16),
                        pltpu.VMEM((tm, ML_OUT), BF16),
                        pltpu.VMEM((tm, ML_OUT), BF16),
                        pltpu.VMEM((8, tm), F32),
                        pltpu.VMEM((ML_HEADS, ML_QK, ML_V), F32),
                        pltpu.VMEM((8, ML_QK), F32),
                        pltpu.VMEM((8, LANES), F32)],
        compiler_params=pltpu.CompilerParams(
            dimension_semantics=("arbitrary",), vmem_limit_bytes=V7X_VMEM_LIMIT_BYTES),
        name="proj_mlstm",
    )(x2, pos_col, invf, w_all, wg_t, gbias, qg, kvg, wuq_all, wuk, wuv, head_g, beta_ml)


def _scores(qt, k):
    return _dot(k, qt)


def _chunk_causal(s):
    t = s.shape[0]
    kc = lax.broadcasted_iota(jnp.int32, (t, t), 0) // CHUNK
    qc = lax.broadcasted_iota(jnp.int32, (t, t), 1) // CHUNK
    return jnp.where(kc <= qc, s, -jnp.inf)


def _softmax_update(state, s, vt, smax=None):
    m_prev, l_prev, acc = state
    if smax is None:
        smax = jnp.max(s, axis=0, keepdims=True)
    m_new = jnp.maximum(m_prev, smax)
    alpha = jnp.exp2(m_prev - m_new)
    p = jnp.exp2(s - m_new)
    l_new = alpha * l_prev + jnp.sum(p, axis=0, keepdims=True)
    acc_new = alpha * acc + _dot(vt, p.astype(BF16))
    return m_new, l_new, acc_new


def _flash_kernel(q_ref, k_ref, vt_ref, beta_ref, qn_ref, kf_ref, o_ref, m_sc, l_sc, acc_sc, s_sc,
                  mx_sc):
    i = pl.program_id(2)
    t = ATT_T
    m_sc[...] = jnp.full(m_sc.shape, NEG_BIG, F32)
    l_sc[...] = jnp.zeros(l_sc.shape, F32)
    acc_sc[...] = jnp.zeros(acc_sc.shape, F32)

    def keys(j):
        return k_ref[pl.ds(pl.multiple_of(j * t, t), t), :]

    def load(half):
        return m_sc[half], l_sc[half], acc_sc[half]

    def store(half, state):
        m_sc[half], l_sc[half], acc_sc[half] = state

    def put(tile, scores):
        s_sc[tile] = scores
        mx_sc[tile] = jnp.max(scores, axis=0, keepdims=True)

    q0 = q_ref[:, 0:t]
    q1 = q_ref[:, t:2 * t]

    first_step = (pl.program_id(0) == 0) & (pl.program_id(1) == 0) & (i == 0)

    @pl.when(first_step)
    def _():
        kn0, kn1 = keys(0), keys(1)
        put(0, _scores(q0, kn0))
        put(1, _scores(q1, kn0))
        put(2, _scores(q0, kn1))
        put(3, _scores(q1, kn1))

    def body(jj, carry):
        v0, v1 = vt_ref[2 * jj], vt_ref[2 * jj + 1]
        kn0, kn1 = keys(2 * jj + 2), keys(2 * jj + 3)
        sn0 = _scores(q0, kn0)
        sn1 = _scores(q1, kn0)
        st0 = _softmax_update(load(0), s_sc[0], v0, mx_sc[0])
        put(0, sn0)
        st1 = _softmax_update(load(1), s_sc[1], v0, mx_sc[1])
        put(1, sn1)
        sn2 = _scores(q0, kn1)
        store(0, _softmax_update(st0, s_sc[2], v1, mx_sc[2]))
        put(2, sn2)
        store(1, _softmax_update(st1, s_sc[3], v1, mx_sc[3]))
        put(3, _scores(q1, kn1))
        return carry

    def body2(u, carry):
        body(2 * u, carry)
        return body(2 * u + 1, carry)

    lax.fori_loop(0, i // 2, body2, 0)

    @pl.when(i % 2 == 1)
    def _():
        body(i - 1, 0)

    v0, v1 = vt_ref[2 * i], vt_ref[2 * i + 1]
    qn0, qn1 = qn_ref[:, 0:t], qn_ref[:, t:2 * t]
    kf0, kf1 = kf_ref[0:t, :], kf_ref[t:2 * t, :]
    sn0 = _scores(qn0, kf0)
    sn1 = _scores(qn1, kf0)
    st0 = _softmax_update(load(0), _chunk_causal(s_sc[0]), v0)
    put(0, sn0)
    st1 = _softmax_update(load(1), s_sc[1], v0, mx_sc[1])
    put(1, sn1)
    st1 = _softmax_update(st1, _chunk_causal(s_sc[3]), v1)
    sn2 = _scores(qn0, kf1)
    sn3 = _scores(qn1, kf1)
    for half, (_, l_fin, acc_fin) in enumerate((st0, st1)):
        out = jnp.transpose(acc_fin / l_fin) * beta_ref[...]
        o_ref[half * t:(half + 1) * t, :] = out.astype(BF16)
    put(2, sn2)
    put(3, sn3)


def _flash_call(qt_pad, k_pad, vt, beta_mla, batch, seq):
    T = k_pad.shape[0]
    t = ATT_T
    tq = 2 * t
    nq = seq // tq
    nkv = seq // t
    n_steps = batch * MLA_HEADS * nq

    def successor(b, h, i):
        g = jnp.minimum((b * MLA_HEADS + h) * nq + i + 1, n_steps - 1)
        return g // (nq * MLA_HEADS), (g // nq) % MLA_HEADS, g % nq

    def next_q(b, h, i):
        b2, h2, i2 = successor(b, h, i)
        return h2, b2 * nq + i2

    def next_first_keys(b, h, i):
        b2, h2, _ = successor(b, h, i)
        return b2 * (seq // tq), h2

    return pl.pallas_call(
        _flash_kernel,
        grid=(batch, MLA_HEADS, nq),
        in_specs=[
            pl.BlockSpec((HEAD_PAD, tq), lambda b, h, i: (h, b * nq + i)),
            pl.BlockSpec((seq, HEAD_PAD), lambda b, h, i: (b, h)),
            pl.BlockSpec((nkv, MLA_V, t), lambda b, h, i: (b, h, 0)),
            pl.BlockSpec((1, MLA_V), lambda b, h, i: (0, h)),
            pl.BlockSpec((HEAD_PAD, tq), next_q),
            pl.BlockSpec((tq, HEAD_PAD), next_first_keys),
        ],
        out_specs=pl.BlockSpec((tq, MLA_V), lambda b, h, i: (b * nq + i, h)),
        out_shape=jax.ShapeDtypeStruct((T, MLA_OUT), BF16),
        scratch_shapes=[pltpu.VMEM((2, 1, t), F32), pltpu.VMEM((2, 1, t), F32),
                        pltpu.VMEM((2, MLA_V, t), F32), pltpu.VMEM((4, t, t), F32),
                        pltpu.VMEM((4, 1, t), F32)],
        compiler_params=pltpu.CompilerParams(
            dimension_semantics=("arbitrary", "arbitrary", "arbitrary"),
            vmem_limit_bytes=V7X_VMEM_LIMIT_BYTES),
        name="mla_flash",
    )(qt_pad, k_pad, vt, beta_mla, qt_pad, k_pad)


def _layernorm(y, g, b):
    mu = jnp.mean(y, axis=-1, keepdims=True)
    yc = y - mu
    return yc * lax.rsqrt(jnp.mean(yc * yc, axis=-1, keepdims=True) + LN_EPS) * g + b


def _outproj_kernel(a_ref, m_ref, x_ref, w_ref, g_ref, b_ref, o_ref):
    tm = o_ref.shape[0]
    sub = tm // OUT_SUBTILES

    def rows(r):
        return slice(r * sub, (r + 1) * sub)

    def project(r):
        return _dot(a_ref[rows(r), :], w_ref[0:MLA_OUT, :]) + _dot(m_ref[rows(r), :], w_ref[MLA_OUT:, :])

    def finish(r, acc):
        o_ref[rows(r), :] = _layernorm(DN_ALPHA * x_ref[rows(r), :] + acc, g_ref[...], b_ref[...])

    acc = project(0)
    for r in range(1, OUT_SUBTILES):
        nxt = project(r)
        finish(r - 1, acc)
        acc = nxt
    finish(OUT_SUBTILES - 1, acc)


def _outproj_call(attn, mem, x2, w_out, g, b):
    T = x2.shape[0]
    tm = OUT_TM
    return pl.pallas_call(
        _outproj_kernel,
        grid=(T // tm,),
        in_specs=[
            pl.BlockSpec((tm, MLA_OUT), lambda i: (i, 0)),
            pl.BlockSpec((tm, ML_OUT), lambda i: (i, 0)),
            pl.BlockSpec((tm, D_MODEL), lambda i: (i, 0)),
            _const_spec((MLA_OUT + ML_OUT, D_MODEL)),
            _const_spec((1, D_MODEL)),
            _const_spec((1, D_MODEL)),
        ],
        out_specs=pl.BlockSpec((tm, D_MODEL), lambda i: (i, 0)),
        out_shape=jax.ShapeDtypeStruct((T, D_MODEL), F32),
        compiler_params=pltpu.CompilerParams(
            dimension_semantics=("arbitrary",), vmem_limit_bytes=V7X_VMEM_LIMIT_BYTES),
        name="outproj_ln",
    )(attn, mem, x2, w_out, g, b)


def _gelu_tanh(x):
    c = math.sqrt(2.0 / math.pi)
    return 0.5 * x * (1.0 + jnp.tanh(c * (x + 0.044715 * (x * x * x))))


def _ffn_kernel(tiles_per_seq, h_ref, wg_ref, wv_ref, wd_ref, cw_ref, cb_ref, g_ref, b_ref,
                o_ref, hb_ref, gbuf_ref, val_ref, halo_ref):
    i = pl.program_id(0)
    f = pl.program_id(1)
    nf = pl.num_programs(1) - 1
    tm = FFN_TM
    cols = [slice(c * FFN_SUB, (c + 1) * FFN_SUB) for c in range(FFN_TF // FFN_SUB)]
    first = (i % tiles_per_seq) == 0

    def up_project(c):
        hb = hb_ref[...]
        return _dot(hb, wg_ref[:, c]), _dot(hb, wv_ref[:, c])

    def stash(c, gate, val):
        gbuf_ref[0:8, c] = jnp.where(first, 0.0, halo_ref[f, :, c])
        gbuf_ref[8:8 + tm, c] = gate
        halo_ref[f, :, c] = gate[tm - 8:tm, :]
        val_ref[:, c] = val

    def down_project(c, r0=0, nr=tm):
        conv = (cw_ref[0:1, c] * gbuf_ref[6 + r0:6 + r0 + nr, c]
                + cw_ref[1:2, c] * gbuf_ref[7 + r0:7 + r0 + nr, c]
                + cw_ref[2:3, c] * gbuf_ref[8 + r0:8 + r0 + nr, c] + cb_ref[:, c])
        act = (_gelu_tanh(conv) * val_ref[r0:r0 + nr, c]).astype(BF16)
        return _dot(act, wd_ref[c, :])

    def consume(r0=0, nr=tm):
        acc = None
        for c in cols:
            part = down_project(c, r0, nr)
            acc = part if acc is None else acc + part
        return o_ref[r0:r0 + nr, :] + acc

    @pl.when(jnp.logical_and(i == 0, f == 0))
    def _():
        halo_ref[...] = jnp.zeros(halo_ref.shape, F32)

    @pl.when(f == 0)
    def _():
        hb_ref[...] = h_ref[...].astype(BF16)
        o_ref[...] = DN_ALPHA * h_ref[...]
        for c in cols:
            stash(c, *up_project(c))

    @pl.when(jnp.logical_and(f > 0, f < nf))
    def _():
        ups = [up_project(c) for c in cols]
        o_ref[...] = consume()
        for c, (gate, val) in zip(cols, ups):
            stash(c, gate, val)

    @pl.when(f == nf)
    def _():
        nr = tm // FFN_LAST_SUBTILES
        for r in range(FFN_LAST_SUBTILES):
            rows = slice(r * nr, (r + 1) * nr)
            o_ref[rows, :] = _layernorm(consume(r * nr, nr), g_ref[...], b_ref[...])


def _ffn_call(h1, wg, wv, wd, conv_w, conv_b, g, b, seq):
    T = h1.shape[0]
    tm, tf = FFN_TM, FFN_TF
    nf = D_FF // tf
    kern = functools.partial(_ffn_kernel, seq // tm)
    produced = lambda i, f: (0, jnp.minimum(f, nf - 1))
    consumed = lambda i, f: (0, jnp.maximum(f - 1, 0))
    return pl.pallas_call(
        kern,
        grid=(T // tm, nf + 1),
        in_specs=[
            pl.BlockSpec((tm, D_MODEL), lambda i, f: (i, 0)),
            pl.BlockSpec((D_MODEL, tf), produced),
            pl.BlockSpec((D_MODEL, tf), produced),
            pl.BlockSpec((tf, D_MODEL), lambda i, f: (jnp.maximum(f - 1, 0), 0)),
            pl.BlockSpec((3, tf), consumed),
            pl.BlockSpec((1, tf), consumed),
            _const_spec((1, D_MODEL)),
            _const_spec((1, D_MODEL)),
        ],
        out_specs=pl.BlockSpec((tm, D_MODEL), lambda i, f: (i, 0)),
        out_shape=jax.ShapeDtypeStruct((T, D_MODEL), F32),
        scratch_shapes=[pltpu.VMEM((tm, D_MODEL), BF16),
                        pltpu.VMEM((tm + 8, tf), F32),
                        pltpu.VMEM((tm, tf), F32),
                        pltpu.VMEM((nf, 8, tf), F32)],
        compiler_params=pltpu.CompilerParams(
            dimension_semantics=("arbitrary", "arbitrary"),
            vmem_limit_bytes=V7X_VMEM_LIMIT_BYTES),
        name="convglu_ffn_ln",
    )(h1, wg, wv, wd, conv_w, conv_b, g, b)


def _prep_proj_weights(w_in, w_uq, b_igate, b_fgate):
    half = MLA_ROPE // 2
    o = 0
    w_cq = w_in[:, o:o + Q_LORA]; o += Q_LORA
    w_ckv = w_in[:, o:o + KV_LORA]; o += KV_LORA
    w_kr = w_in[:, o:o + MLA_ROPE]; o += MLA_ROPE
    w_mq = w_in[:, o:o + ML_HEADS * ML_QK]; o += ML_HEADS * ML_QK
    w_mk = w_in[:, o:o + ML_HEADS * ML_QK]; o += ML_HEADS * ML_QK
    w_mv = w_in[:, o:o + ML_OUT]; o += ML_OUT
    w_mo = w_in[:, o:o + ML_OUT]; o += ML_OUT
    w_gates = w_in[:, o:o + 2 * ML_HEADS]
    w_krot = jnp.concatenate([-w_kr[:, half:], w_kr[:, :half]], axis=1)
    z64 = jnp.zeros((D_MODEL, LANES - MLA_ROPE), w_in.dtype)
    w_all = jnp.concatenate([w_cq, w_ckv, w_kr, z64, w_krot, z64, w_mq, w_mk, w_mv, w_mo], axis=1)
    wg_t = w_gates.T
    gbias = jnp.concatenate([b_igate, b_fgate])[:, None]

    uq = w_uq.reshape(Q_LORA, MLA_HEADS, MLA_NOPE + MLA_ROPE)
    uq_nope = uq[:, :, :MLA_NOPE].reshape(Q_LORA, MLA_HEADS * MLA_NOPE)
    uq_r = uq[:, :, MLA_NOPE:]
    uq_rot = jnp.concatenate([-uq_r[:, :, half:], uq_r[:, :, :half]], axis=2)
    zpad = jnp.zeros((Q_LORA, MLA_HEADS, LANES - MLA_ROPE), w_uq.dtype)
    uq_r = jnp.concatenate([uq_r, zpad], axis=2).reshape(Q_LORA, MLA_HEADS * LANES)
    uq_rot = jnp.concatenate([uq_rot, zpad], axis=2).reshape(Q_LORA, MLA_HEADS * LANES)
    wuq_all = jnp.concatenate([uq_nope, uq_r, uq_rot], axis=1).T
    return w_all.astype(BF16), wg_t.astype(BF16), gbias.astype(F32), wuq_all.astype(BF16)


def kernel(x, positions, w_in, q_norm_g, kv_norm_g, w_uq, w_uk, w_uv, b_igate, b_fgate,
           ml_head_g, beta_mla, beta_ml, w_out, ln1_g, ln1_b, w_ffn_gate, w_ffn_val,
           conv_w, conv_b, w_down, ln2_g, ln2_b):
    B, S, D = x.shape
    T = B * S
    assert D == D_MODEL and w_in.shape[0] == DEPTH == 1
    assert S % (2 * ATT_T) == 0 and S % ML_L == 0 and S % FFN_TM == 0 and T % PROJ_TM == 0
    l = 0
    x2 = x.reshape(T, D)
    pos_col = positions.reshape(T, 1).astype(F32)
    inv_freq = 1.0 / (ROPE_BASE ** (jnp.arange(0, MLA_ROPE, 2, dtype=F32) / MLA_ROPE))
    invf = jnp.tile(inv_freq, LANES // (MLA_ROPE // 2))[None, :]

    w_all, wg_t, gbias, wuq_all = _prep_proj_weights(w_in[l], w_uq[l], b_igate[l], b_fgate[l])
    qt_pad, k_pad, vt, mem = _proj_mlstm_call(
        x2, pos_col, invf, w_all, wg_t, gbias,
        q_norm_g[l][None, :], kv_norm_g[l][None, :], wuq_all,
        w_uk[l].astype(BF16), w_uv[l].T.astype(BF16),
        ml_head_g[l][None, :], beta_ml[l][None, :], S)

    attn = _flash_call(qt_pad, k_pad, vt, beta_mla[l][None, :], B, S)
    h1 = _outproj_call(attn, mem, x2, w_out[l].astype(BF16), ln1_g[l][None, :], ln1_b[l][None, :])
    out = _ffn_call(h1, w_ffn_gate[l].astype(BF16), w_ffn_val[l].astype(BF16),
                    w_down[l].astype(BF16), conv_w[l], conv_b[l][None, :],
                    ln2_g[l][None, :], ln2_b[l][None, :], S)
    return out.reshape(B, S, D)
```

```python
import functools
import math

import jax
import jax.numpy as jnp
from jax import lax
from jax.experimental import pallas as pl
from jax.experimental.pallas import tpu as pltpu

D_MODEL = 2048
CHUNK = 64
MLA_HEADS = 8
MLA_NOPE = 128
MLA_ROPE = 64
MLA_V = 128
Q_LORA = 512
KV_LORA = 256
ROPE_BASE = 10000.0
MLA_OUT = MLA_HEADS * MLA_V
ML_HEADS = 4
ML_QK = 128
ML_V = 256
ML_OUT = ML_HEADS * ML_V
GATE_CAP = 15.0
D_FF = 5632
RMS_EPS = 1e-6
LN_EPS = 1e-5
DEPTH = 1
DN_ALPHA = (2 * DEPTH) ** 0.25

LANES = 128
V7X_VMEM_LIMIT_BYTES = 58 * 1024 * 1024

PROJ_TM = 256
ATT_T = 512
ML_L = 256
OUT_TM = 512
OUT_SUBTILES = 4
FFN_TM = 512
FFN_TF = 512
FFN_SUB = 256
FFN_LAST_SUBTILES = 2

HEAD_PAD = 2 * LANES
PROJ_W = 4096
_C_CQ, _C_CKV, _C_KR, _C_KROT, _C_MQ, _C_MK, _C_MV, _C_MO = 0, 512, 768, 896, 1024, 1536, 2048, 3072

F32 = jnp.float32
BF16 = jnp.bfloat16
NEG_BIG = -1e30


def _nt_dot(a, b):
    return lax.dot_general(a, b, (((1,), (1,)), ((), ())), preferred_element_type=F32)


def _dot(a, b):
    return jnp.dot(a, b, preferred_element_type=F32)


def _const_spec(shape):
    nd = len(shape)
    return pl.BlockSpec(shape, lambda *_: (0,) * nd, pipeline_mode=pl.Buffered(1))


def _lane_cumsum(x):
    n = x.shape[-1]
    lane = lax.broadcasted_iota(jnp.int32, x.shape, x.ndim - 1)
    d = 1
    while d < n:
        x = x + jnp.where(lane >= d, pltpu.roll(x, d, x.ndim - 1), 0.0)
        d *= 2
    return x


def _proj_mlstm_kernel(chunks_per_seq,
                       x_ref, pos_ref, invf_ref, w_ref, wg_ref, gb_ref, qg_ref, kvg_ref,
                       wuq_ref, wuk_ref, wuv_ref, hg_ref, beta_ref,
                       q_ref, k_ref, vt_ref, mem_ref,
                       xb_ref, mq_s, mk_s, mv_s, mo_s, gt_s, c_sc, n_sc, m_sc):
    g = pl.program_id(0)
    n = pl.num_programs(0) - 1
    L = ML_L
    heads = range(ML_HEADS)
    pj = {}
    ml = {}

    def proj_latents():
        xb_ref[...] = x_ref[...].astype(BF16)
        xb = xb_ref[...]
        pj["cq"] = _dot(xb, w_ref[:, _C_CQ:_C_CQ + Q_LORA])
        pj["p1"] = _dot(xb, w_ref[:, _C_CKV:_C_MQ])
        pj["mq"] = _dot(xb, w_ref[:, _C_MQ:_C_MK]).astype(BF16)
        pj["mk"] = (_dot(xb, w_ref[:, _C_MK:_C_MV]) * (ML_QK ** -0.5)).astype(BF16)

    def proj_norms():
        ang = pos_ref[...] * invf_ref[...]
        pj["cos"] = jnp.cos(ang)
        pj["sin"] = jnp.sin(ang)
        cq, p1 = pj["cq"], pj["p1"]
        cq = cq * lax.rsqrt(jnp.mean(cq * cq, axis=-1, keepdims=True) + RMS_EPS) * qg_ref[...]
        pj["cqb"] = cq.astype(BF16)
        ckv = p1[:, 0:KV_LORA]
        ckv = ckv * lax.rsqrt(jnp.mean(ckv * ckv, axis=-1, keepdims=True) + RMS_EPS) * kvg_ref[...]
        pj["ckvb"] = ckv.astype(BF16)
        pj["k_rope"] = (p1[:, 256:384] * pj["cos"] + p1[:, 384:512] * pj["sin"]).astype(BF16)

    def proj_up():
        qscale = (MLA_NOPE + MLA_ROPE) ** -0.5 * math.log2(math.e)
        cqb, ckvb = pj["cqb"], pj["ckvb"]
        pj["qt_nope"] = _nt_dot(wuq_ref[0:1024, :], cqb) * qscale
        pj["qt_r"] = _nt_dot(wuq_ref[1024:2048, :], cqb)
        pj["qt_rot"] = _nt_dot(wuq_ref[2048:3072, :], cqb)
        pj["k_nope"] = _dot(ckvb, wuk_ref[...]).astype(BF16)
        vt_ref[0] = _nt_dot(wuv_ref[...], ckvb).astype(BF16)
        pj["qscale"] = qscale

    def proj_mlstm_inputs():
        xb = xb_ref[...]
        pj["mv"] = _dot(xb, w_ref[:, _C_MV:_C_MO]).astype(BF16)
        pj["mo"] = _dot(xb, w_ref[:, _C_MO:PROJ_W]).astype(BF16)
        pj["gt"] = _nt_dot(wg_ref[...], xb) + gb_ref[...]

    def proj_write():
        cos_t = jnp.transpose(pj["cos"])
        sin_t = jnp.transpose(pj["sin"])
        for h in range(MLA_HEADS):
            rows = slice(h * LANES, (h + 1) * LANES)
            q_ref[h * HEAD_PAD:h * HEAD_PAD + LANES, :] = pj["qt_nope"][rows, :].astype(BF16)
            q_ref[h * HEAD_PAD + LANES:(h + 1) * HEAD_PAD, :] = (
                (pj["qt_r"][rows, :] * cos_t + pj["qt_rot"][rows, :] * sin_t) * pj["qscale"]).astype(BF16)
            k_ref[:, h * HEAD_PAD:h * HEAD_PAD + LANES] = pj["k_nope"][:, h * LANES:(h + 1) * LANES]
            k_ref[:, h * HEAD_PAD + LANES:(h + 1) * HEAD_PAD] = pj["k_rope"]
        mq_s[...] = pj["mq"]
        mk_s[...] = pj["mk"]
        mv_s[...] = pj["mv"]
        mo_s[...] = pj["mo"]
        gt_s[...] = pj["gt"]

    def scan_scores():
        ml["qs"] = [mq_s[:, h * ML_QK:(h + 1) * ML_QK] for h in heads]
        ml["ks"] = [mk_s[:, h * ML_QK:(h + 1) * ML_QK] for h in heads]
        ml["vs"] = [mv_s[:, h * ML_V:(h + 1) * ML_V] for h in heads]
        ml["cts"] = [c_sc[h] for h in heads]
        ml["qk"] = [_nt_dot(ml["qs"][h], ml["ks"][h]) for h in heads]
        ml["qc"] = [_dot(ml["qs"][h], ml["cts"][h].astype(BF16)) for h in heads]

    def scan_heads():
        g8 = gt_s[...]
        capped = GATE_CAP * jnp.tanh(g8 / GATE_CAP)
        logf = jnp.minimum(capped, 0.0) - jnp.log1p(jnp.exp(-jnp.abs(capped)))
        row8 = lax.broadcasted_iota(jnp.int32, g8.shape, 0)
        bsum = _lane_cumsum(jnp.where(row8 >= ML_HEADS, logf, 0.0))
        gb8 = jnp.where(row8 < ML_HEADS, capped - pltpu.roll(bsum, ML_HEADS, 0), bsum)
        rr = lax.broadcasted_iota(jnp.int32, (L, L), 0)
        cc = lax.broadcasted_iota(jnp.int32, (L, L), 1)
        causal = cc <= rr
        eye = cc == rr
        pending = []
        for h in heads:
            g_row = gb8[h:h + 1, :]
            b_row = gb8[ML_HEADS + h:ML_HEADS + h + 1, :]
            G = jnp.broadcast_to(g_row, (L, L))
            Bm = jnp.broadcast_to(b_row, (L, L))
            m_prev = m_sc[h:h + 1, 0:1]
            cummax = jnp.max(jnp.where(causal, G, -jnp.inf), axis=1, keepdims=True)
            mcol = jnp.maximum(m_prev, cummax)
            g_col = jnp.sum(jnp.where(eye, G, 0.0), axis=1, keepdims=True)
            b_col = jnp.sum(jnp.where(eye, Bm, 0.0), axis=1, keepdims=True)
            dmat = jnp.where(causal, jnp.exp(G - mcol), 0.0)
            inter_w = jnp.exp(m_prev - mcol)

            qh, kh, vh, ct = ml["qs"][h], ml["ks"][h], ml["vs"][h], ml["cts"][h]
            nrow = n_sc[h:h + 1, :]

            s = ml["qk"][h] * dmat
            num = _dot(s.astype(BF16), vh) + inter_w * ml["qc"][h]
            den = (jnp.sum(s, axis=1, keepdims=True)
                   + inter_w * jnp.sum(qh.astype(F32) * nrow, axis=1, keepdims=True))
            hval = num / jnp.maximum(jnp.abs(den), jnp.exp(-(b_col + mcol)))

            m_last = mcol[L - 1:L, :]
            decay = inter_w[L - 1:L, :]
            w_col = jnp.exp(g_col - m_last)
            kw = kh.astype(F32) * w_col
            pending.append((kw.astype(BF16), vh, decay, ct))
            n_sc[h:h + 1, :] = decay * nrow + jnp.sum(kw, axis=0, keepdims=True)
            m_sc[h:h + 1, :] = jnp.broadcast_to(b_row[:, L - 1:L] + m_last, (1, LANES))

            hn = hval * lax.rsqrt(jnp.mean(hval * hval, axis=-1, keepdims=True) + RMS_EPS)
            hn = hn * hg_ref[:, h * ML_V:(h + 1) * ML_V]
            og = jax.nn.sigmoid(mo_s[:, h * ML_V:(h + 1) * ML_V].astype(F32))
            mem_ref[:, h * ML_V:(h + 1) * ML_V] = (
                og * hn * beta_ref[:, h * ML_V:(h + 1) * ML_V]).astype(BF16)
        ml["pending"] = pending

    def scan_update():
        for h, (kwb, vh, decay, ct) in enumerate(ml["pending"]):
            upd = lax.dot_general(kwb, vh, (((0,), (0,)), ((), ())),
                                  preferred_element_type=F32)
            c_sc[h] = decay * ct + upd

    @pl.when(jnp.logical_and(g > 0, (g - 1) % chunks_per_seq == 0))
    def _():
        c_sc[...] = jnp.zeros(c_sc.shape, F32)
        n_sc[...] = jnp.zeros(n_sc.shape, F32)
        m_sc[...] = jnp.full(m_sc.shape, -jnp.inf, F32)

    @pl.when(g == 0)
    def _():
        proj_latents()
        proj_norms()
        proj_up()
        proj_mlstm_inputs()
        proj_write()

    @pl.when(jnp.logical_and(g > 0, g < n))
    def _():
        scan_scores()
        proj_latents()
        proj_norms()
        scan_heads()
        proj_up()
        scan_update()
        proj_mlstm_inputs()
        proj_write()

    @pl.when(g == n)
    def _():
        scan_scores()
        scan_heads()
        scan_update()


def _proj_mlstm_call(x2, pos_col, invf, w_all, wg_t, gbias, qg, kvg, wuq_all, wuk, wuv,
                     head_g, beta_ml, seq):
    T = x2.shape[0]
    tm = PROJ_TM
    assert tm == ML_L
    n = T // tm
    cur = lambda g: jnp.minimum(g, n - 1)
    row = lambda w: pl.BlockSpec((tm, w), lambda g: (cur(g), 0))
    out_shapes = (
        jax.ShapeDtypeStruct((MLA_HEADS * HEAD_PAD, T), BF16),
        jax.ShapeDtypeStruct((T, MLA_HEADS * HEAD_PAD), BF16),
        jax.ShapeDtypeStruct((T // ATT_T, MLA_OUT, ATT_T), BF16),
        jax.ShapeDtypeStruct((T, ML_OUT), BF16),
    )
    kern = functools.partial(_proj_mlstm_kernel, seq // ML_L)
    return pl.pallas_call(
        kern,
        grid=(n + 1,),
        in_specs=[
            row(D_MODEL),
            pl.BlockSpec((tm, 1), lambda g: (cur(g), 0)),
            _const_spec((1, LANES)),
            _const_spec((D_MODEL, PROJ_W)),
            _const_spec((8, D_MODEL)),
            _const_spec((8, 1)),
            _const_spec((1, Q_LORA)),
            _const_spec((1, KV_LORA)),
            _const_spec((3072, Q_LORA)),
            _const_spec((KV_LORA, MLA_HEADS * MLA_NOPE)),
            _const_spec((MLA_OUT, KV_LORA)),
            _const_spec((1, ML_OUT)),
            _const_spec((1, ML_OUT)),
        ],
        out_specs=(
            pl.BlockSpec((MLA_HEADS * HEAD_PAD, tm), lambda g: (0, cur(g))),
            row(MLA_HEADS * HEAD_PAD),
            pl.BlockSpec((1, MLA_OUT, tm),
                         lambda g: (cur(g) // (ATT_T // tm), 0, cur(g) % (ATT_T // tm))),
            pl.BlockSpec((tm, ML_OUT), lambda g: (jnp.maximum(g - 1, 0), 0)),
        ),
        out_shape=out_shapes,
        scratch_shapes=[pltpu.VMEM((tm, D_MODEL), BF16),
                        pltpu.VMEM((tm, ML_HEADS * ML_QK), BF16),
                        pltpu.VMEM((tm, ML_HEADS * ML_QK), BF16),
                        pltpu.VMEM((tm, ML_OUT), BF16),
                        pltpu.VMEM((tm, ML_OUT), BF16),
                        pltpu.VMEM((8, tm), F32),
                        pltpu.VMEM((ML_HEADS, ML_QK, ML_V), F32),
                        pltpu.VMEM((8, ML_QK), F32),
                        pltpu.VMEM((8, LANES), F32)],
        compiler_params=pltpu.CompilerParams(
            dimension_semantics=("arbitrary",), vmem_limit_bytes=V7X_VMEM_LIMIT_BYTES),
        name="proj_mlstm",
    )(x2, pos_col, invf, w_all, wg_t, gbias, qg, kvg, wuq_all, wuk, wuv, head_g, beta_ml)


def _scores(qt, k):
    return _dot(k, qt)


def _chunk_causal(s):
    t = s.shape[0]
    kc = lax.broadcasted_iota(jnp.int32, (t, t), 0) // CHUNK
    qc = lax.broadcasted_iota(jnp.int32, (t, t), 1) // CHUNK
    return jnp.where(kc <= qc, s, -jnp.inf)


def _softmax_update(state, s, vt, smax=None):
    m_prev, l_prev, acc = state
    if smax is None:
        smax = jnp.max(s, axis=0, keepdims=True)
    m_new = jnp.maximum(m_prev, smax)
    alpha = jnp.exp2(m_prev - m_new)
    p = jnp.exp2(s - m_new)
    l_new = alpha * l_prev + jnp.sum(p, axis=0, keepdims=True)
    acc_new = alpha * acc + _dot(vt, p.astype(BF16))
    return m_new, l_new, acc_new


def _flash_kernel(q_ref, k_ref, vt_ref, beta_ref, qn_ref, kf_ref, o_ref, m_sc, l_sc, acc_sc, s_sc,
                  mx_sc):
    i = pl.program_id(2)
    t = ATT_T
    m_sc[...] = jnp.full(m_sc.shape, NEG_BIG, F32)
    l_sc[...] = jnp.zeros(l_sc.shape, F32)
    acc_sc[...] = jnp.zeros(acc_sc.shape, F32)

    def keys(j):
        return k_ref[pl.ds(pl.multiple_of(j * t, t), t), :]

    def load(half):
        return m_sc[half], l_sc[half], acc_sc[half]

    def store(half, state):
        m_sc[half], l_sc[half], acc_sc[half] = state

    def put(tile, scores):
        s_sc[tile] = scores
        mx_sc[tile] = jnp.max(scores, axis=0, keepdims=True)

    q0 = q_ref[:, 0:t]
    q1 = q_ref[:, t:2 * t]

    first_step = (pl.program_id(0) == 0) & (pl.program_id(1) == 0) & (i == 0)

    @pl.when(first_step)
    def _():
        kn0, kn1 = keys(0), keys(1)
        put(0, _scores(q0, kn0))
        put(1, _scores(q1, kn0))
        put(2, _scores(q0, kn1))
        put(3, _scores(q1, kn1))

    def body(jj, carry):
        v0, v1 = vt_ref[2 * jj], vt_ref[2 * jj + 1]
        kn0, kn1 = keys(2 * jj + 2), keys(2 * jj + 3)
        sn0 = _scores(q0, kn0)
        sn1 = _scores(q1, kn0)
        st0 = _softmax_update(load(0), s_sc[0], v0, mx_sc[0])
        put(0, sn0)
        st1 = _softmax_update(load(1), s_sc[1], v0, mx_sc[1])
        put(1, sn1)
        sn2 = _scores(q0, kn1)
        store(0, _softmax_update(st0, s_sc[2], v1, mx_sc[2]))
        put(2, sn2)
        store(1, _softmax_update(st1, s_sc[3], v1, mx_sc[3]))
        put(3, _scores(q1, kn1))
        return carry

    def body2(u, carry):
        body(2 * u, carry)
        return body(2 * u + 1, carry)

    lax.fori_loop(0, i // 2, body2, 0)

    @pl.when(i % 2 == 1)
    def _():
        body(i - 1, 0)

    v0, v1 = vt_ref[2 * i], vt_ref[2 * i + 1]
    qn0, qn1 = qn_ref[:, 0:t], qn_ref[:, t:2 * t]
    kf0, kf1 = kf_ref[0:t, :], kf_ref[t:2 * t, :]
    sn0 = _scores(qn0, kf0)
    sn1 = _scores(qn1, kf0)
    st0 = _softmax_update(load(0), _chunk_causal(s_sc[0]), v0)
    put(0, sn0)
    st1 = _softmax_update(load(1), s_sc[1], v0, mx_sc[1])
    put(1, sn1)
    st1 = _softmax_update(st1, _chunk_causal(s_sc[3]), v1)
    sn2 = _scores(qn0, kf1)
    sn3 = _scores(qn1, kf1)
    for half, (_, l_fin, acc_fin) in enumerate((st0, st1)):
        out = jnp.transpose(acc_fin / l_fin) * beta_ref[...]
        o_ref[half * t:(half + 1) * t, :] = out.astype(BF16)
    put(2, sn2)
    put(3, sn3)


def _flash_call(qt_pad, k_pad, vt, beta_mla, batch, seq):
    T = k_pad.shape[0]
    t = ATT_T
    tq = 2 * t
    nq = seq // tq
    nkv = seq // t
    n_steps = batch * MLA_HEADS * nq

    def successor(b, h, i):
        g = jnp.minimum((b * MLA_HEADS + h) * nq + i + 1, n_steps - 1)
        return g // (nq * MLA_HEADS), (g // nq) % MLA_HEADS, g % nq

    def next_q(b, h, i):
        b2, h2, i2 = successor(b, h, i)
        return h2, b2 * nq + i2

    def next_first_keys(b, h, i):
        b2, h2, _ = successor(b, h, i)
        return b2 * (seq // tq), h2

    return pl.pallas_call(
        _flash_kernel,
        grid=(batch, MLA_HEADS, nq),
        in_specs=[
            pl.BlockSpec((HEAD_PAD, tq), lambda b, h, i: (h, b * nq + i)),
            pl.BlockSpec((seq, HEAD_PAD), lambda b, h, i: (b, h)),
            pl.BlockSpec((nkv, MLA_V, t), lambda b, h, i: (b, h, 0)),
            pl.BlockSpec((1, MLA_V), lambda b, h, i: (0, h)),
            pl.BlockSpec((HEAD_PAD, tq), next_q),
            pl.BlockSpec((tq, HEAD_PAD), next_first_keys),
        ],
        out_specs=pl.BlockSpec((tq, MLA_V), lambda b, h, i: (b * nq + i, h)),
        out_shape=jax.ShapeDtypeStruct((T, MLA_OUT), BF16),
        scratch_shapes=[pltpu.VMEM((2, 1, t), F32), pltpu.VMEM((2, 1, t), F32),
                        pltpu.VMEM((2, MLA_V, t), F32), pltpu.VMEM((4, t, t), F32),
                        pltpu.VMEM((4, 1, t), F32)],
        compiler_params=pltpu.CompilerParams(
            dimension_semantics=("arbitrary", "arbitrary", "arbitrary"),
            vmem_limit_bytes=V7X_VMEM_LIMIT_BYTES),
        name="mla_flash",
    )(qt_pad, k_pad, vt, beta_mla, qt_pad, k_pad)


def _layernorm(y, g, b):
    mu = jnp.mean(y, axis=-1, keepdims=True)
    yc = y - mu
    return yc * lax.rsqrt(jnp.mean(yc * yc, axis=-1, keepdims=True) + LN_EPS) * g + b


def _outproj_kernel(a_ref, m_ref, x_ref, w_ref, g_ref, b_ref, wg32_ref, wv32_ref, wd32_ref,
                    o_ref, wgb_ref, wvb_ref, wdb_ref):
    wgb_ref[...] = wg32_ref[...].astype(BF16)
    wvb_ref[...] = wv32_ref[...].astype(BF16)
    wdb_ref[...] = wd32_ref[...].astype(BF16)

    tm = o_ref.shape[0]
    sub = tm // OUT_SUBTILES

    def rows(r):
        return slice(r * sub, (r + 1) * sub)

    def project(r):
        return _dot(a_ref[rows(r), :], w_ref[0:MLA_OUT, :]) + _dot(m_ref[rows(r), :], w_ref[MLA_OUT:, :])

    def finish(r, acc):
        o_ref[rows(r), :] = _layernorm(DN_ALPHA * x_ref[rows(r), :] + acc, g_ref[...], b_ref[...])

    acc = project(0)
    for r in range(1, OUT_SUBTILES):
        nxt = project(r)
        finish(r - 1, acc)
        acc = nxt
    finish(OUT_SUBTILES - 1, acc)


def _outproj_call(attn, mem, x2, w_out, g, b, wg32, wv32, wd32):
    T = x2.shape[0]
    tm = OUT_TM
    n = T // tm
    assert D_MODEL % n == 0 and D_FF % n == 0
    up_rows, down_rows = D_MODEL // n, D_FF // n
    slab = lambda rows, cols: pl.BlockSpec((rows, cols), lambda i: (i, 0))
    return pl.pallas_call(
        _outproj_kernel,
        grid=(n,),
        in_specs=[
            pl.BlockSpec((tm, MLA_OUT), lambda i: (i, 0)),
            pl.BlockSpec((tm, ML_OUT), lambda i: (i, 0)),
            pl.BlockSpec((tm, D_MODEL), lambda i: (i, 0)),
            _const_spec((MLA_OUT + ML_OUT, D_MODEL)),
            _const_spec((1, D_MODEL)),
            _const_spec((1, D_MODEL)),
            slab(up_rows, D_FF), slab(up_rows, D_FF), slab(down_rows, D_MODEL),
        ],
        out_specs=(pl.BlockSpec((tm, D_MODEL), lambda i: (i, 0)),
                   slab(up_rows, D_FF), slab(up_rows, D_FF), slab(down_rows, D_MODEL)),
        out_shape=(jax.ShapeDtypeStruct((T, D_MODEL), F32),
                   jax.ShapeDtypeStruct((D_MODEL, D_FF), BF16),
                   jax.ShapeDtypeStruct((D_MODEL, D_FF), BF16),
                   jax.ShapeDtypeStruct((D_FF, D_MODEL), BF16)),
        compiler_params=pltpu.CompilerParams(
            dimension_semantics=("arbitrary",), vmem_limit_bytes=V7X_VMEM_LIMIT_BYTES),
        name="outproj_ln",
    )(attn, mem, x2, w_out, g, b, wg32, wv32, wd32)


def _gelu_tanh(x):
    c = math.sqrt(2.0 / math.pi)
    return 0.5 * x * (1.0 + jnp.tanh(c * (x + 0.044715 * (x * x * x))))


def _ffn_kernel(tiles_per_seq, h_ref, wg_ref, wv_ref, wd_ref, cw_ref, cb_ref, g_ref, b_ref,
                o_ref, hb_ref, gbuf_ref, val_ref, halo_ref):
    i = pl.program_id(0)
    f = pl.program_id(1)
    nf = pl.num_programs(1) - 1
    tm = FFN_TM
    cols = [slice(c * FFN_SUB, (c + 1) * FFN_SUB) for c in range(FFN_TF // FFN_SUB)]
    first = (i % tiles_per_seq) == 0

    def up_project(c):
        hb = hb_ref[...]
        return _dot(hb, wg_ref[:, c]), _dot(hb, wv_ref[:, c])

    def stash(c, gate, val):
        gbuf_ref[0:8, c] = jnp.where(first, 0.0, halo_ref[f, :, c])
        gbuf_ref[8:8 + tm, c] = gate
        halo_ref[f, :, c] = gate[tm - 8:tm, :]
        val_ref[:, c] = val

    def down_project(c, r0=0, nr=tm):
        conv = (cw_ref[0:1, c] * gbuf_ref[6 + r0:6 + r0 + nr, c]
                + cw_ref[1:2, c] * gbuf_ref[7 + r0:7 + r0 + nr, c]
                + cw_ref[2:3, c] * gbuf_ref[8 + r0:8 + r0 + nr, c] + cb_ref[:, c])
        act = (_gelu_tanh(conv) * val_ref[r0:r0 + nr, c]).astype(BF16)
        return _dot(act, wd_ref[c, :])

    def consume(r0=0, nr=tm):
        acc = None
        for c in cols:
            part = down_project(c, r0, nr)
            acc = part if acc is None else acc + part
        return o_ref[r0:r0 + nr, :] + acc

    @pl.when(jnp.logical_and(i == 0, f == 0))
    def _():
        halo_ref[...] = jnp.zeros(halo_ref.shape, F32)

    @pl.when(f == 0)
    def _():
        hb_ref[...] = h_ref[...].astype(BF16)
        o_ref[...] = DN_ALPHA * h_ref[...]
        for c in cols:
            stash(c, *up_project(c))

    @pl.when(jnp.logical_and(f > 0, f < nf))
    def _():
        ups = [up_project(c) for c in cols]
        o_ref[...] = consume()
        for c, (gate, val) in zip(cols, ups):
            stash(c, gate, val)

    @pl.when(f == nf)
    def _():
        nr = tm // FFN_LAST_SUBTILES
        for r in range(FFN_LAST_SUBTILES):
            rows = slice(r * nr, (r + 1) * nr)
            o_ref[rows, :] = _layernorm(consume(r * nr, nr), g_ref[...], b_ref[...])


def _ffn_call(h1, wg, wv, wd, conv_w, conv_b, g, b, seq):
    T = h1.shape[0]
    tm, tf = FFN_TM, FFN_TF
    nf = D_FF // tf
    kern = functools.partial(_ffn_kernel, seq // tm)
    produced = lambda i, f: (0, jnp.minimum(f, nf - 1))
    consumed = lambda i, f: (0, jnp.maximum(f - 1, 0))
    return pl.pallas_call(
        kern,
        grid=(T // tm, nf + 1),
        in_specs=[
            pl.BlockSpec((tm, D_MODEL), lambda i, f: (i, 0)),
            pl.BlockSpec((D_MODEL, tf), produced),
            pl.BlockSpec((D_MODEL, tf), produced),
            pl.BlockSpec((tf, D_MODEL), lambda i, f: (jnp.maximum(f - 1, 0), 0)),
            pl.BlockSpec((3, tf), consumed),
            pl.BlockSpec((1, tf), consumed),
            _const_spec((1, D_MODEL)),
            _const_spec((1, D_MODEL)),
        ],
        out_specs=pl.BlockSpec((tm, D_MODEL), lambda i, f: (i, 0)),
        out_shape=jax.ShapeDtypeStruct((T, D_MODEL), F32),
        scratch_shapes=[pltpu.VMEM((tm, D_MODEL), BF16),
                        pltpu.VMEM((tm + 8, tf), F32),
                        pltpu.VMEM((tm, tf), F32),
                        pltpu.VMEM((nf, 8, tf), F32)],
        compiler_params=pltpu.CompilerParams(
            dimension_semantics=("arbitrary", "arbitrary"),
            vmem_limit_bytes=V7X_VMEM_LIMIT_BYTES),
        name="convglu_ffn_ln",
    )(h1, wg, wv, wd, conv_w, conv_b, g, b)


def _prep_proj_weights(w_in, w_uq, b_igate, b_fgate):
    half = MLA_ROPE // 2
    o = 0
    w_cq = w_in[:, o:o + Q_LORA]; o += Q_LORA
    w_ckv = w_in[:, o:o + KV_LORA]; o += KV_LORA
    w_kr = w_in[:, o:o + MLA_ROPE]; o += MLA_ROPE
    w_mq = w_in[:, o:o + ML_HEADS * ML_QK]; o += ML_HEADS * ML_QK
    w_mk = w_in[:, o:o + ML_HEADS * ML_QK]; o += ML_HEADS * ML_QK
    w_mv = w_in[:, o:o + ML_OUT]; o += ML_OUT
    w_mo = w_in[:, o:o + ML_OUT]; o += ML_OUT
    w_gates = w_in[:, o:o + 2 * ML_HEADS]
    w_krot = jnp.concatenate([-w_kr[:, half:], w_kr[:, :half]], axis=1)
    z64 = jnp.zeros((D_MODEL, LANES - MLA_ROPE), w_in.dtype)
    w_all = jnp.concatenate([w_cq, w_ckv, w_kr, z64, w_krot, z64, w_mq, w_mk, w_mv, w_mo], axis=1)
    wg_t = w_gates.T
    gbias = jnp.concatenate([b_igate, b_fgate])[:, None]

    uq = w_uq.reshape(Q_LORA, MLA_HEADS, MLA_NOPE + MLA_ROPE)
    uq_nope = uq[:, :, :MLA_NOPE].reshape(Q_LORA, MLA_HEADS * MLA_NOPE)
    uq_r = uq[:, :, MLA_NOPE:]
    uq_rot = jnp.concatenate([-uq_r[:, :, half:], uq_r[:, :, :half]], axis=2)
    zpad = jnp.zeros((Q_LORA, MLA_HEADS, LANES - MLA_ROPE), w_uq.dtype)
    uq_r = jnp.concatenate([uq_r, zpad], axis=2).reshape(Q_LORA, MLA_HEADS * LANES)
    uq_rot = jnp.concatenate([uq_rot, zpad], axis=2).reshape(Q_LORA, MLA_HEADS * LANES)
    wuq_all = jnp.concatenate([uq_nope, uq_r, uq_rot], axis=1).T
    return w_all.astype(BF16), wg_t.astype(BF16), gbias.astype(F32), wuq_all.astype(BF16)


def kernel(x, positions, w_in, q_norm_g, kv_norm_g, w_uq, w_uk, w_uv, b_igate, b_fgate,
           ml_head_g, beta_mla, beta_ml, w_out, ln1_g, ln1_b, w_ffn_gate, w_ffn_val,
           conv_w, conv_b, w_down, ln2_g, ln2_b):
    B, S, D = x.shape
    T = B * S
    assert D == D_MODEL and w_in.shape[0] == DEPTH == 1
    assert S % (2 * ATT_T) == 0 and S % ML_L == 0 and S % FFN_TM == 0 and T % PROJ_TM == 0
    l = 0
    x2 = x.reshape(T, D)
    pos_col = positions.reshape(T, 1).astype(F32)
    inv_freq = 1.0 / (ROPE_BASE ** (jnp.arange(0, MLA_ROPE, 2, dtype=F32) / MLA_ROPE))
    invf = jnp.tile(inv_freq, LANES // (MLA_ROPE // 2))[None, :]

    w_all, wg_t, gbias, wuq_all = _prep_proj_weights(w_in[l], w_uq[l], b_igate[l], b_fgate[l])
    qt_pad, k_pad, vt, mem = _proj_mlstm_call(
        x2, pos_col, invf, w_all, wg_t, gbias,
        q_norm_g[l][None, :], kv_norm_g[l][None, :], wuq_all,
        w_uk[l].astype(BF16), w_uv[l].T.astype(BF16),
        ml_head_g[l][None, :], beta_ml[l][None, :], S)

    attn = _flash_call(qt_pad, k_pad, vt, beta_mla[l][None, :], B, S)
    h1, wg_b, wv_b, wd_b = _outproj_call(
        attn, mem, x2, w_out[l].astype(BF16), ln1_g[l][None, :], ln1_b[l][None, :],
        w_ffn_gate[l], w_ffn_val[l], w_down[l])
    out = _ffn_call(h1, wg_b, wv_b, wd_b, conv_w[l], conv_b[l][None, :],
                    ln2_g[l][None, :], ln2_b[l][None, :], S)
    return out.reshape(B, S, D)
```

```python
import functools
import math

import jax
import jax.numpy as jnp
from jax import lax
from jax.experimental import pallas as pl
from jax.experimental.pallas import tpu as pltpu

D_MODEL = 2048
CHUNK = 64
MLA_HEADS = 8
MLA_NOPE = 128
MLA_ROPE = 64
MLA_V = 128
Q_LORA = 512
KV_LORA = 256
ROPE_BASE = 10000.0
MLA_OUT = MLA_HEADS * MLA_V
ML_HEADS = 4
ML_QK = 128
ML_V = 256
ML_OUT = ML_HEADS * ML_V
GATE_CAP = 15.0
D_FF = 5632
RMS_EPS = 1e-6
LN_EPS = 1e-5
DEPTH = 1
DN_ALPHA = (2 * DEPTH) ** 0.25

LANES = 128
V7X_VMEM_LIMIT_BYTES = 58 * 1024 * 1024

PROJ_TM = 256
ATT_T = 512
ML_L = 256
OUT_TM = 512
OUT_SUBTILES = 4
FFN_TM = 512
FFN_TF = 512
FFN_SUB = 256
FFN_LAST_SUBTILES = 2

HEAD_PAD = 2 * LANES
PROJ_W = 4096
_C_CQ, _C_CKV, _C_KR, _C_KROT, _C_MQ, _C_MK, _C_MV, _C_MO = 0, 512, 768, 896, 1024, 1536, 2048, 3072

F32 = jnp.float32
BF16 = jnp.bfloat16
NEG_BIG = -1e30


def _nt_dot(a, b):
    return lax.dot_general(a, b, (((1,), (1,)), ((), ())), preferred_element_type=F32)


def _dot(a, b):
    return jnp.dot(a, b, preferred_element_type=F32)


def _const_spec(shape):
    nd = len(shape)
    return pl.BlockSpec(shape, lambda *_: (0,) * nd, pipeline_mode=pl.Buffered(1))


def _lane_cumsum(x):
    n = x.shape[-1]
    lane = lax.broadcasted_iota(jnp.int32, x.shape, x.ndim - 1)
    d = 1
    while d < n:
        x = x + jnp.where(lane >= d, pltpu.roll(x, d, x.ndim - 1), 0.0)
        d *= 2
    return x


def _proj_mlstm_kernel(chunks_per_seq,
                       x_ref, pos_ref, invf_ref, w_ref, wg_ref, gb_ref, qg_ref, kvg_ref,
                       wuq_ref, wuk_ref, wuv_ref, hg_ref, beta_ref,
                       q_ref, k_ref, vt_ref, mem_ref,
                       xb_ref, mq_s, mk_s, mv_s, mo_s, gt_s, c_sc, n_sc, m_sc):
    g = pl.program_id(0)
    n = pl.num_programs(0) - 1
    L = ML_L
    heads = range(ML_HEADS)
    pj = {}
    ml = {}

    def proj_latents():
        xb_ref[...] = x_ref[...].astype(BF16)
        xb = xb_ref[...]
        pj["cq"] = _dot(xb, w_ref[:, _C_CQ:_C_CQ + Q_LORA])
        pj["p1"] = _dot(xb, w_ref[:, _C_CKV:_C_MQ])
        pj["mq"] = _dot(xb, w_ref[:, _C_MQ:_C_MK]).astype(BF16)
        pj["mk"] = (_dot(xb, w_ref[:, _C_MK:_C_MV]) * (ML_QK ** -0.5)).astype(BF16)

    def proj_norms():
        ang = pos_ref[...] * invf_ref[...]
        pj["cos"] = jnp.cos(ang)
        pj["sin"] = jnp.sin(ang)
        cq, p1 = pj["cq"], pj["p1"]
        cq = cq * lax.rsqrt(jnp.mean(cq * cq, axis=-1, keepdims=True) + RMS_EPS) * qg_ref[...]
        pj["cqb"] = cq.astype(BF16)
        ckv = p1[:, 0:KV_LORA]
        ckv = ckv * lax.rsqrt(jnp.mean(ckv * ckv, axis=-1, keepdims=True) + RMS_EPS) * kvg_ref[...]
        pj["ckvb"] = ckv.astype(BF16)
        pj["k_rope"] = (p1[:, 256:384] * pj["cos"] + p1[:, 384:512] * pj["sin"]).astype(BF16)

    def proj_up():
        qscale = (MLA_NOPE + MLA_ROPE) ** -0.5 * math.log2(math.e)
        cqb, ckvb = pj["cqb"], pj["ckvb"]
        pj["qt_nope"] = _nt_dot(wuq_ref[0:1024, :], cqb) * qscale
        pj["qt_r"] = _nt_dot(wuq_ref[1024:2048, :], cqb)
        pj["qt_rot"] = _nt_dot(wuq_ref[2048:3072, :], cqb)
        pj["k_nope"] = _dot(ckvb, wuk_ref[...]).astype(BF16)
        vt_ref[0] = _nt_dot(wuv_ref[...], ckvb).astype(BF16)
        pj["qscale"] = qscale

    def proj_mlstm_inputs():
        xb = xb_ref[...]
        pj["mv"] = _dot(xb, w_ref[:, _C_MV:_C_MO]).astype(BF16)
        pj["mo"] = _dot(xb, w_ref[:, _C_MO:PROJ_W]).astype(BF16)
        pj["gt"] = _nt_dot(wg_ref[...], xb) + gb_ref[...]

    def proj_write():
        cos_t = jnp.transpose(pj["cos"])
        sin_t = jnp.transpose(pj["sin"])
        for h in range(MLA_HEADS):
            rows = slice(h * LANES, (h + 1) * LANES)
            q_ref[h * HEAD_PAD:h * HEAD_PAD + LANES, :] = pj["qt_nope"][rows, :].astype(BF16)
            q_ref[h * HEAD_PAD + LANES:(h + 1) * HEAD_PAD, :] = (
                (pj["qt_r"][rows, :] * cos_t + pj["qt_rot"][rows, :] * sin_t) * pj["qscale"]).astype(BF16)
            k_ref[:, h * HEAD_PAD:h * HEAD_PAD + LANES] = pj["k_nope"][:, h * LANES:(h + 1) * LANES]
            k_ref[:, h * HEAD_PAD + LANES:(h + 1) * HEAD_PAD] = pj["k_rope"]
        mq_s[...] = pj["mq"]
        mk_s[...] = pj["mk"]
        mv_s[...] = pj["mv"]
        mo_s[...] = pj["mo"]
        gt_s[...] = pj["gt"]

    def scan_scores():
        ml["qs"] = [mq_s[:, h * ML_QK:(h + 1) * ML_QK] for h in heads]
        ml["ks"] = [mk_s[:, h * ML_QK:(h + 1) * ML_QK] for h in heads]
        ml["vs"] = [mv_s[:, h * ML_V:(h + 1) * ML_V] for h in heads]
        ml["cts"] = [c_sc[h] for h in heads]
        ml["qk"] = [_nt_dot(ml["qs"][h], ml["ks"][h]) for h in heads]
        ml["qc"] = [_dot(ml["qs"][h], ml["cts"][h].astype(BF16)) for h in heads]

    def scan_heads():
        g8 = gt_s[...]
        capped = GATE_CAP * jnp.tanh(g8 / GATE_CAP)
        logf = jnp.minimum(capped, 0.0) - jnp.log1p(jnp.exp(-jnp.abs(capped)))
        row8 = lax.broadcasted_iota(jnp.int32, g8.shape, 0)
        bsum = _lane_cumsum(jnp.where(row8 >= ML_HEADS, logf, 0.0))
        gb8 = jnp.where(row8 < ML_HEADS, capped - pltpu.roll(bsum, ML_HEADS, 0), bsum)
        rr = lax.broadcasted_iota(jnp.int32, (L, L), 0)
        cc = lax.broadcasted_iota(jnp.int32, (L, L), 1)
        causal = cc <= rr
        eye = cc == rr
        pending = []
        for h in heads:
            g_row = gb8[h:h + 1, :]
            b_row = gb8[ML_HEADS + h:ML_HEADS + h + 1, :]
            G = jnp.broadcast_to(g_row, (L, L))
            Bm = jnp.broadcast_to(b_row, (L, L))
            m_prev = m_sc[h:h + 1, 0:1]
            cummax = jnp.max(jnp.where(causal, G, -jnp.inf), axis=1, keepdims=True)
            mcol = jnp.maximum(m_prev, cummax)
            g_col = jnp.sum(jnp.where(eye, G, 0.0), axis=1, keepdims=True)
            b_col = jnp.sum(jnp.where(eye, Bm, 0.0), axis=1, keepdims=True)
            dmat = jnp.where(causal, jnp.exp(G - mcol), 0.0)
            inter_w = jnp.exp(m_prev - mcol)

            qh, kh, vh, ct = ml["qs"][h], ml["ks"][h], ml["vs"][h], ml["cts"][h]
            nrow = n_sc[h:h + 1, :]

            s = ml["qk"][h] * dmat
            num = _dot(s.astype(BF16), vh) + inter_w * ml["qc"][h]
            den = (jnp.sum(s, axis=1, keepdims=True)
                   + inter_w * jnp.sum(qh.astype(F32) * nrow, axis=1, keepdims=True))
            hval = num / jnp.maximum(jnp.abs(den), jnp.exp(-(b_col + mcol)))

            m_last = mcol[L - 1:L, :]
            decay = inter_w[L - 1:L, :]
            w_col = jnp.exp(g_col - m_last)
            kw = kh.astype(F32) * w_col
            pending.append((kw.astype(BF16), vh, decay, ct))
            n_sc[h:h + 1, :] = decay * nrow + jnp.sum(kw, axis=0, keepdims=True)
            m_sc[h:h + 1, :] = jnp.broadcast_to(b_row[:, L - 1:L] + m_last, (1, LANES))

            hn = hval * lax.rsqrt(jnp.mean(hval * hval, axis=-1, keepdims=True) + RMS_EPS)
            hn = hn * hg_ref[:, h * ML_V:(h + 1) * ML_V]
            og = jax.nn.sigmoid(mo_s[:, h * ML_V:(h + 1) * ML_V].astype(F32))
            mem_ref[:, h * ML_V:(h + 1) * ML_V] = (
                og * hn * beta_ref[:, h * ML_V:(h + 1) * ML_V]).astype(BF16)
        ml["pending"] = pending

    def scan_update():
        for h, (kwb, vh, decay, ct) in enumerate(ml["pending"]):
            upd = lax.dot_general(kwb, vh, (((0,), (0,)), ((), ())),
                                  preferred_element_type=F32)
            c_sc[h] = decay * ct + upd

    @pl.when(jnp.logical_and(g > 0, (g - 1) % chunks_per_seq == 0))
    def _():
        c_sc[...] = jnp.zeros(c_sc.shape, F32)
        n_sc[...] = jnp.zeros(n_sc.shape, F32)
        m_sc[...] = jnp.full(m_sc.shape, -jnp.inf, F32)

    @pl.when(g == 0)
    def _():
        proj_latents()
        proj_norms()
        proj_up()
        proj_mlstm_inputs()
        proj_write()

    @pl.when(jnp.logical_and(g > 0, g < n))
    def _():
        scan_scores()
        proj_latents()
        proj_mlstm_inputs()
        proj_norms()
        scan_heads()
        proj_up()
        scan_update()
        proj_write()

    @pl.when(g == n)
    def _():
        scan_scores()
        scan_heads()
        scan_update()


def _proj_mlstm_call(x2, pos_col, invf, w_all, wg_t, gbias, qg, kvg, wuq_all, wuk, wuv,
                     head_g, beta_ml, seq):
    T = x2.shape[0]
    tm = PROJ_TM
    assert tm == ML_L
    n = T // tm
    cur = lambda g: jnp.minimum(g, n - 1)
    row = lambda w: pl.BlockSpec((tm, w), lambda g: (cur(g), 0))
    out_shapes = (
        jax.ShapeDtypeStruct((MLA_HEADS * HEAD_PAD, T), BF16),
        jax.ShapeDtypeStruct((T, MLA_HEADS * HEAD_PAD), BF16),
        jax.ShapeDtypeStruct((T // ATT_T, MLA_OUT, ATT_T), BF16),
        jax.ShapeDtypeStruct((T, ML_OUT), BF16),
    )
    kern = functools.partial(_proj_mlstm_kernel, seq // ML_L)
    return pl.pallas_call(
        kern,
        grid=(n + 1,),
        in_specs=[
            row(D_MODEL),
            pl.BlockSpec((tm, 1), lambda g: (cur(g), 0)),
            _const_spec((1, LANES)),
            _const_spec((D_MODEL, PROJ_W)),
            _const_spec((8, D_MODEL)),
            _const_spec((8, 1)),
            _const_spec((1, Q_LORA)),
            _const_spec((1, KV_LORA)),
            _const_spec((3072, Q_LORA)),
            _const_spec((KV_LORA, MLA_HEADS * MLA_NOPE)),
            _const_spec((MLA_OUT, KV_LORA)),
            _const_spec((1, ML_OUT)),
            _const_spec((1, ML_OUT)),
        ],
        out_specs=(
            pl.BlockSpec((MLA_HEADS * HEAD_PAD, tm), lambda g: (0, cur(g))),
            row(MLA_HEADS * HEAD_PAD),
            pl.BlockSpec((1, MLA_OUT, tm),
                         lambda g: (cur(g) // (ATT_T // tm), 0, cur(g) % (ATT_T // tm))),
            pl.BlockSpec((tm, ML_OUT), lambda g: (jnp.maximum(g - 1, 0), 0)),
        ),
        out_shape=out_shapes,
        scratch_shapes=[pltpu.VMEM((tm, D_MODEL), BF16),
                        pltpu.VMEM((tm, ML_HEADS * ML_QK), BF16),
                        pltpu.VMEM((tm, ML_HEADS * ML_QK), BF16),
                        pltpu.VMEM((tm, ML_OUT), BF16),
                        pltpu.VMEM((tm, ML_OUT), BF16),
                        pltpu.VMEM((8, tm), F32),
                        pltpu.VMEM((ML_HEADS, ML_QK, ML_V), F32),
                        pltpu.VMEM((8, ML_QK), F32),
                        pltpu.VMEM((8, LANES), F32)],
        compiler_params=pltpu.CompilerParams(
            dimension_semantics=("arbitrary",), vmem_limit_bytes=V7X_VMEM_LIMIT_BYTES),
        name="proj_mlstm",
    )(x2, pos_col, invf, w_all, wg_t, gbias, qg, kvg, wuq_all, wuk, wuv, head_g, beta_ml)


def _scores(qt, k):
    return _dot(k, qt)


def _chunk_causal(s):
    t = s.shape[0]
    kc = lax.broadcasted_iota(jnp.int32, (t, t), 0) // CHUNK
    qc = lax.broadcasted_iota(jnp.int32, (t, t), 1) // CHUNK
    return jnp.where(kc <= qc, s, -jnp.inf)


def _softmax_update(state, s, vt, smax=None):
    m_prev, l_prev, acc = state
    if smax is None:
        smax = jnp.max(s, axis=0, keepdims=True)
    m_new = jnp.maximum(m_prev, smax)
    alpha = jnp.exp2(m_prev - m_new)
    p = jnp.exp2(s - m_new)
    l_new = alpha * l_prev + jnp.sum(p, axis=0, keepdims=True)
    acc_new = alpha * acc + _dot(vt, p.astype(BF16))
    return m_new, l_new, acc_new


def _flash_kernel(q_ref, k_ref, vt_ref, beta_ref, qn_ref, kf_ref, o_ref, m_sc, l_sc, acc_sc, s_sc,
                  mx_sc):
    i = pl.program_id(2)
    t = ATT_T
    m_sc[...] = jnp.full(m_sc.shape, NEG_BIG, F32)
    l_sc[...] = jnp.zeros(l_sc.shape, F32)
    acc_sc[...] = jnp.zeros(acc_sc.shape, F32)

    def keys(j):
        return k_ref[pl.ds(pl.multiple_of(j * t, t), t), :]

    def load(half):
        return m_sc[half], l_sc[half], acc_sc[half]

    def store(half, state):
        m_sc[half], l_sc[half], acc_sc[half] = state

    def put(tile, scores):
        s_sc[tile] = scores
        mx_sc[tile] = jnp.max(scores, axis=0, keepdims=True)

    q0 = q_ref[:, 0:t]
    q1 = q_ref[:, t:2 * t]

    first_step = (pl.program_id(0) == 0) & (pl.program_id(1) == 0) & (i == 0)

    @pl.when(first_step)
    def _():
        kn0, kn1 = keys(0), keys(1)
        put(0, _scores(q0, kn0))
        put(1, _scores(q1, kn0))
        put(2, _scores(q0, kn1))
        put(3, _scores(q1, kn1))

    def body(jj, carry):
        v0, v1 = vt_ref[2 * jj], vt_ref[2 * jj + 1]
        kn0, kn1 = keys(2 * jj + 2), keys(2 * jj + 3)
        sn0 = _scores(q0, kn0)
        sn1 = _scores(q1, kn0)
        st0 = _softmax_update(load(0), s_sc[0], v0, mx_sc[0])
        put(0, sn0)
        st1 = _softmax_update(load(1), s_sc[1], v0, mx_sc[1])
        put(1, sn1)
        sn2 = _scores(q0, kn1)
        store(0, _softmax_update(st0, s_sc[2], v1, mx_sc[2]))
        put(2, sn2)
        store(1, _softmax_update(st1, s_sc[3], v1, mx_sc[3]))
        put(3, _scores(q1, kn1))
        return carry

    def body2(u, carry):
        body(2 * u, carry)
        return body(2 * u + 1, carry)

    lax.fori_loop(0, i // 2, body2, 0)

    @pl.when(i % 2 == 1)
    def _():
        body(i - 1, 0)

    v0, v1 = vt_ref[2 * i], vt_ref[2 * i + 1]
    qn0, qn1 = qn_ref[:, 0:t], qn_ref[:, t:2 * t]
    kf0, kf1 = kf_ref[0:t, :], kf_ref[t:2 * t, :]
    sn0 = _scores(qn0, kf0)
    sn1 = _scores(qn1, kf0)
    st0 = _softmax_update(load(0), _chunk_causal(s_sc[0]), v0)
    put(0, sn0)
    st1 = _softmax_update(load(1), s_sc[1], v0, mx_sc[1])
    put(1, sn1)
    st1 = _softmax_update(st1, _chunk_causal(s_sc[3]), v1)
    sn2 = _scores(qn0, kf1)
    sn3 = _scores(qn1, kf1)
    for half, (_, l_fin, acc_fin) in enumerate((st0, st1)):
        out = jnp.transpose(acc_fin / l_fin) * beta_ref[...]
        o_ref[half * t:(half + 1) * t, :] = out.astype(BF16)
    put(2, sn2)
    put(3, sn3)


def _flash_call(qt_pad, k_pad, vt, beta_mla, batch, seq):
    T = k_pad.shape[0]
    t = ATT_T
    tq = 2 * t
    nq = seq // tq
    nkv = seq // t
    n_steps = batch * MLA_HEADS * nq

    def successor(b, h, i):
        g = jnp.minimum((b * MLA_HEADS + h) * nq + i + 1, n_steps - 1)
        return g // (nq * MLA_HEADS), (g // nq) % MLA_HEADS, g % nq

    def next_q(b, h, i):
        b2, h2, i2 = successor(b, h, i)
        return h2, b2 * nq + i2

    def next_first_keys(b, h, i):
        b2, h2, _ = successor(b, h, i)
        return b2 * (seq // tq), h2

    return pl.pallas_call(
        _flash_kernel,
        grid=(batch, MLA_HEADS, nq),
        in_specs=[
            pl.BlockSpec((HEAD_PAD, tq), lambda b, h, i: (h, b * nq + i)),
            pl.BlockSpec((seq, HEAD_PAD), lambda b, h, i: (b, h)),
            pl.BlockSpec((nkv, MLA_V, t), lambda b, h, i: (b, h, 0)),
            pl.BlockSpec((1, MLA_V), lambda b, h, i: (0, h)),
            pl.BlockSpec((HEAD_PAD, tq), next_q),
            pl.BlockSpec((tq, HEAD_PAD), next_first_keys),
        ],
        out_specs=pl.BlockSpec((tq, MLA_V), lambda b, h, i: (b * nq + i, h)),
        out_shape=jax.ShapeDtypeStruct((T, MLA_OUT), BF16),
        scratch_shapes=[pltpu.VMEM((2, 1, t), F32), pltpu.VMEM((2, 1, t), F32),
                        pltpu.VMEM((2, MLA_V, t), F32), pltpu.VMEM((4, t, t), F32),
                        pltpu.VMEM((4, 1, t), F32)],
        compiler_params=pltpu.CompilerParams(
            dimension_semantics=("arbitrary", "arbitrary", "arbitrary"),
            vmem_limit_bytes=V7X_VMEM_LIMIT_BYTES),
        name="mla_flash",
    )(qt_pad, k_pad, vt, beta_mla, qt_pad, k_pad)


def _layernorm(y, g, b):
    mu = jnp.mean(y, axis=-1, keepdims=True)
    yc = y - mu
    return yc * lax.rsqrt(jnp.mean(yc * yc, axis=-1, keepdims=True) + LN_EPS) * g + b


def _outproj_kernel(a_ref, m_ref, x_ref, w_ref, g_ref, b_ref, wg32_ref, wv32_ref, wd32_ref,
                    o_ref, wgb_ref, wvb_ref, wdb_ref):
    wgb_ref[...] = wg32_ref[...].astype(BF16)
    wvb_ref[...] = wv32_ref[...].astype(BF16)
    wdb_ref[...] = wd32_ref[...].astype(BF16)

    tm = o_ref.shape[0]
    sub = tm // OUT_SUBTILES

    def rows(r):
        return slice(r * sub, (r + 1) * sub)

    def project(r):
        return _dot(a_ref[rows(r), :], w_ref[0:MLA_OUT, :]) + _dot(m_ref[rows(r), :], w_ref[MLA_OUT:, :])

    def finish(r, acc):
        o_ref[rows(r), :] = _layernorm(DN_ALPHA * x_ref[rows(r), :] + acc, g_ref[...], b_ref[...])

    acc = project(0)
    for r in range(1, OUT_SUBTILES):
        nxt = project(r)
        finish(r - 1, acc)
        acc = nxt
    finish(OUT_SUBTILES - 1, acc)


def _outproj_call(attn, mem, x2, w_out, g, b, wg32, wv32, wd32):
    T = x2.shape[0]
    tm = OUT_TM
    n = T // tm
    assert D_MODEL % n == 0 and D_FF % n == 0
    up_rows, down_rows = D_MODEL // n, D_FF // n
    slab = lambda rows, cols: pl.BlockSpec((rows, cols), lambda i: (i, 0))
    return pl.pallas_call(
        _outproj_kernel,
        grid=(n,),
        in_specs=[
            pl.BlockSpec((tm, MLA_OUT), lambda i: (i, 0)),
            pl.BlockSpec((tm, ML_OUT), lambda i: (i, 0)),
            pl.BlockSpec((tm, D_MODEL), lambda i: (i, 0)),
            _const_spec((MLA_OUT + ML_OUT, D_MODEL)),
            _const_spec((1, D_MODEL)),
            _const_spec((1, D_MODEL)),
            slab(up_rows, D_FF), slab(up_rows, D_FF), slab(down_rows, D_MODEL),
        ],
        out_specs=(pl.BlockSpec((tm, D_MODEL), lambda i: (i, 0)),
                   slab(up_rows, D_FF), slab(up_rows, D_FF), slab(down_rows, D_MODEL)),
        out_shape=(jax.ShapeDtypeStruct((T, D_MODEL), F32),
                   jax.ShapeDtypeStruct((D_MODEL, D_FF), BF16),
                   jax.ShapeDtypeStruct((D_MODEL, D_FF), BF16),
                   jax.ShapeDtypeStruct((D_FF, D_MODEL), BF16)),
        compiler_params=pltpu.CompilerParams(
            dimension_semantics=("arbitrary",), vmem_limit_bytes=V7X_VMEM_LIMIT_BYTES),
        name="outproj_ln",
    )(attn, mem, x2, w_out, g, b, wg32, wv32, wd32)


def _gelu_tanh(x):
    c = math.sqrt(2.0 / math.pi)
    return 0.5 * x * (1.0 + jnp.tanh(c * (x + 0.044715 * (x * x * x))))


def _ffn_kernel(tiles_per_seq, h_ref, wg_ref, wv_ref, wd_ref, cw_ref, cb_ref, g_ref, b_ref,
                o_ref, hb_ref, gbuf_ref, val_ref, halo_ref):
    i = pl.program_id(0)
    f = pl.program_id(1)
    nf = pl.num_programs(1) - 1
    tm = FFN_TM
    cols = [slice(c * FFN_SUB, (c + 1) * FFN_SUB) for c in range(FFN_TF // FFN_SUB)]
    first = (i % tiles_per_seq) == 0

    def up_project(c):
        hb = hb_ref[...]
        return _dot(hb, wg_ref[:, c]), _dot(hb, wv_ref[:, c])

    def stash(c, gate, val):
        gbuf_ref[0:8, c] = jnp.where(first, 0.0, halo_ref[f, :, c])
        gbuf_ref[8:8 + tm, c] = gate
        halo_ref[f, :, c] = gate[tm - 8:tm, :]
        val_ref[:, c] = val

    def down_project(c, r0=0, nr=tm):
        conv = (cw_ref[0:1, c] * gbuf_ref[6 + r0:6 + r0 + nr, c]
                + cw_ref[1:2, c] * gbuf_ref[7 + r0:7 + r0 + nr, c]
                + cw_ref[2:3, c] * gbuf_ref[8 + r0:8 + r0 + nr, c] + cb_ref[:, c])
        act = (_gelu_tanh(conv) * val_ref[r0:r0 + nr, c]).astype(BF16)
        return _dot(act, wd_ref[c, :])

    def consume(r0=0, nr=tm):
        acc = None
        for c in cols:
            part = down_project(c, r0, nr)
            acc = part if acc is None else acc + part
        return o_ref[r0:r0 + nr, :] + acc

    @pl.when(jnp.logical_and(i == 0, f == 0))
    def _():
        halo_ref[...] = jnp.zeros(halo_ref.shape, F32)

    @pl.when(f == 0)
    def _():
        hb_ref[...] = h_ref[...].astype(BF16)
        o_ref[...] = DN_ALPHA * h_ref[...]
        for c in cols:
            stash(c, *up_project(c))

    @pl.when(jnp.logical_and(f > 0, f < nf))
    def _():
        ups = [up_project(c) for c in cols]
        o_ref[...] = consume()
        for c, (gate, val) in zip(cols, ups):
            stash(c, gate, val)

    @pl.when(f == nf)
    def _():
        nr = tm // FFN_LAST_SUBTILES
        for r in range(FFN_LAST_SUBTILES):
            rows = slice(r * nr, (r + 1) * nr)
            o_ref[rows, :] = _layernorm(consume(r * nr, nr), g_ref[...], b_ref[...])


def _ffn_call(h1, wg, wv, wd, conv_w, conv_b, g, b, seq):
    T = h1.shape[0]
    tm, tf = FFN_TM, FFN_TF
    nf = D_FF // tf
    kern = functools.partial(_ffn_kernel, seq // tm)
    produced = lambda i, f: (0, jnp.minimum(f, nf - 1))
    consumed = lambda i, f: (0, jnp.maximum(f - 1, 0))
    return pl.pallas_call(
        kern,
        grid=(T // tm, nf + 1),
        in_specs=[
            pl.BlockSpec((tm, D_MODEL), lambda i, f: (i, 0)),
            pl.BlockSpec((D_MODEL, tf), produced),
            pl.BlockSpec((D_MODEL, tf), produced),
            pl.BlockSpec((tf, D_MODEL), lambda i, f: (jnp.maximum(f - 1, 0), 0)),
            pl.BlockSpec((3, tf), consumed),
            pl.BlockSpec((1, tf), consumed),
            _const_spec((1, D_MODEL)),
            _const_spec((1, D_MODEL)),
        ],
        out_specs=pl.BlockSpec((tm, D_MODEL), lambda i, f: (i, 0)),
        out_shape=jax.ShapeDtypeStruct((T, D_MODEL), F32),
        scratch_shapes=[pltpu.VMEM((tm, D_MODEL), BF16),
                        pltpu.VMEM((tm + 8, tf), F32),
                        pltpu.VMEM((tm, tf), F32),
                        pltpu.VMEM((nf, 8, tf), F32)],
        compiler_params=pltpu.CompilerParams(
            dimension_semantics=("arbitrary", "arbitrary"),
            vmem_limit_bytes=V7X_VMEM_LIMIT_BYTES),
        name="convglu_ffn_ln",
    )(h1, wg, wv, wd, conv_w, conv_b, g, b)


def _prep_proj_weights(w_in, w_uq, b_igate, b_fgate):
    half = MLA_ROPE // 2
    o = 0
    w_cq = w_in[:, o:o + Q_LORA]; o += Q_LORA
    w_ckv = w_in[:, o:o + KV_LORA]; o += KV_LORA
    w_kr = w_in[:, o:o + MLA_ROPE]; o += MLA_ROPE
    w_mq = w_in[:, o:o + ML_HEADS * ML_QK]; o += ML_HEADS * ML_QK
    w_mk = w_in[:, o:o + ML_HEADS * ML_QK]; o += ML_HEADS * ML_QK
    w_mv = w_in[:, o:o + ML_OUT]; o += ML_OUT
    w_mo = w_in[:, o:o + ML_OUT]; o += ML_OUT
    w_gates = w_in[:, o:o + 2 * ML_HEADS]
    w_krot = jnp.concatenate([-w_kr[:, half:], w_kr[:, :half]], axis=1)
    z64 = jnp.zeros((D_MODEL, LANES - MLA_ROPE), w_in.dtype)
    w_all = jnp.concatenate([w_cq, w_ckv, w_kr, z64, w_krot, z64, w_mq, w_mk, w_mv, w_mo], axis=1)
    wg_t = w_gates.T
    gbias = jnp.concatenate([b_igate, b_fgate])[:, None]

    uq = w_uq.reshape(Q_LORA, MLA_HEADS, MLA_NOPE + MLA_ROPE)
    uq_nope = uq[:, :, :MLA_NOPE].reshape(Q_LORA, MLA_HEADS * MLA_NOPE)
    uq_r = uq[:, :, MLA_NOPE:]
    uq_rot = jnp.concatenate([-uq_r[:, :, half:], uq_r[:, :, :half]], axis=2)
    zpad = jnp.zeros((Q_LORA, MLA_HEADS, LANES - MLA_ROPE), w_uq.dtype)
    uq_r = jnp.concatenate([uq_r, zpad], axis=2).reshape(Q_LORA, MLA_HEADS * LANES)
    uq_rot = jnp.concatenate([uq_rot, zpad], axis=2).reshape(Q_LORA, MLA_HEADS * LANES)
    wuq_all = jnp.concatenate([uq_nope, uq_r, uq_rot], axis=1).T
    return w_all.astype(BF16), wg_t.astype(BF16), gbias.astype(F32), wuq_all.astype(BF16)


def kernel(x, positions, w_in, q_norm_g, kv_norm_g, w_uq, w_uk, w_uv, b_igate, b_fgate,
           ml_head_g, beta_mla, beta_ml, w_out, ln1_g, ln1_b, w_ffn_gate, w_ffn_val,
           conv_w, conv_b, w_down, ln2_g, ln2_b):
    B, S, D = x.shape
    T = B * S
    assert D == D_MODEL and w_in.shape[0] == DEPTH == 1
    assert S % (2 * ATT_T) == 0 and S % ML_L == 0 and S % FFN_TM == 0 and T % PROJ_TM == 0
    l = 0
    x2 = x.reshape(T, D)
    pos_col = positions.reshape(T, 1).astype(F32)
    inv_freq = 1.0 / (ROPE_BASE ** (jnp.arange(0, MLA_ROPE, 2, dtype=F32) / MLA_ROPE))
    invf = jnp.tile(inv_freq, LANES // (MLA_ROPE // 2))[None, :]

    w_all, wg_t, gbias, wuq_all = _prep_proj_weights(w_in[l], w_uq[l], b_igate[l], b_fgate[l])
    qt_pad, k_pad, vt, mem = _proj_mlstm_call(
        x2, pos_col, invf, w_all, wg_t, gbias,
        q_norm_g[l][None, :], kv_norm_g[l][None, :], wuq_all,
        w_uk[l].astype(BF16), w_uv[l].T.astype(BF16),
        ml_head_g[l][None, :], beta_ml[l][None, :], S)

    attn = _flash_call(qt_pad, k_pad, vt, beta_mla[l][None, :], B, S)
    h1, wg_b, wv_b, wd_b = _outproj_call(
        attn, mem, x2, w_out[l].astype(BF16), ln1_g[l][None, :], ln1_b[l][None, :],
        w_ffn_gate[l], w_ffn_val[l], w_down[l])
    out = _ffn_call(h1, wg_b, wv_b, wd_b, conv_w[l], conv_b[l][None, :],
                    ln2_g[l][None, :], ln2_b[l][None, :], S)
    return out.reshape(B, S, D)
```

```python
import functools
import math

import jax
import jax.numpy as jnp
from jax import lax
from jax.experimental import pallas as pl
from jax.experimental.pallas import tpu as pltpu

D_MODEL = 2048
CHUNK = 64
MLA_HEADS = 8
MLA_NOPE = 128
MLA_ROPE = 64
MLA_V = 128
Q_LORA = 512
KV_LORA = 256
ROPE_BASE = 10000.0
MLA_OUT = MLA_HEADS * MLA_V
ML_HEADS = 4
ML_QK = 128
ML_V = 256
ML_OUT = ML_HEADS * ML_V
GATE_CAP = 15.0
D_FF = 5632
RMS_EPS = 1e-6
LN_EPS = 1e-5
DEPTH = 1
DN_ALPHA = (2 * DEPTH) ** 0.25

LANES = 128
V7X_VMEM_LIMIT_BYTES = 58 * 1024 * 1024

PROJ_TM = 256
ATT_T = 512
ML_L = 256
OUT_TM = 512
OUT_SUBTILES = 4
FFN_TM = 512
FFN_TF = 512
FFN_SUB = 256
FFN_LAST_SUBTILES = 2

HEAD_PAD = 2 * LANES
PROJ_W = 4096
_C_CQ, _C_CKV, _C_KR, _C_KROT, _C_MQ, _C_MK, _C_MV, _C_MO = 0, 512, 768, 896, 1024, 1536, 2048, 3072
_P1_KR = slice(_C_KR - _C_CKV, _C_KR - _C_CKV + LANES)
_P1_KROT = slice(_C_KROT - _C_CKV, _C_KROT - _C_CKV + LANES)
_UQ_ROWS = MLA_HEADS * LANES
_UQ_NOPE, _UQ_ROPE, _UQ_ROT = (slice(j * _UQ_ROWS, (j + 1) * _UQ_ROWS) for j in range(3))

F32 = jnp.float32
BF16 = jnp.bfloat16
NEG_BIG = -1e30


def _nt_dot(a, b):
    return lax.dot_general(a, b, (((1,), (1,)), ((), ())), preferred_element_type=F32)


def _dot(a, b):
    return jnp.dot(a, b, preferred_element_type=F32)


def _const_spec(shape):
    nd = len(shape)
    return pl.BlockSpec(shape, lambda *_: (0,) * nd, pipeline_mode=pl.Buffered(1))


def _lane_cumsum(x):
    n = x.shape[-1]
    lane = lax.broadcasted_iota(jnp.int32, x.shape, x.ndim - 1)
    d = 1
    while d < n:
        x = x + jnp.where(lane >= d, pltpu.roll(x, d, x.ndim - 1), 0.0)
        d *= 2
    return x


def _proj_mlstm_kernel(chunks_per_seq,
                       x_ref, pos_ref, invf_ref, w_ref, wg_ref, gb_ref, qg_ref, kvg_ref,
                       wuq_ref, wuk_ref, wuv_ref, hg_ref, beta_ref,
                       q_ref, k_ref, vt_ref, mem_ref,
                       xb_ref, mq_s, mk_s, mv_s, mo_s, gt_s, c_sc, n_sc, m_sc):
    g = pl.program_id(0)
    n = pl.num_programs(0) - 1
    L = ML_L
    heads = range(ML_HEADS)
    pj = {}
    ml = {}

    def proj_latents():
        xb_ref[...] = x_ref[...].astype(BF16)
        xb = xb_ref[...]
        pj["cq"] = _dot(xb, w_ref[:, _C_CQ:_C_CQ + Q_LORA])
        pj["p1"] = _dot(xb, w_ref[:, _C_CKV:_C_MQ])
        pj["mq"] = _dot(xb, w_ref[:, _C_MQ:_C_MK]).astype(BF16)
        pj["mk"] = (_dot(xb, w_ref[:, _C_MK:_C_MV]) * (ML_QK ** -0.5)).astype(BF16)

    def proj_norms():
        ang = pos_ref[...] * invf_ref[...]
        pj["cos"] = jnp.cos(ang)
        pj["sin"] = jnp.sin(ang)
        cq, p1 = pj["cq"], pj["p1"]
        cq = cq * lax.rsqrt(jnp.mean(cq * cq, axis=-1, keepdims=True) + RMS_EPS) * qg_ref[...]
        pj["cqb"] = cq.astype(BF16)
        ckv = p1[:, 0:KV_LORA]
        ckv = ckv * lax.rsqrt(jnp.mean(ckv * ckv, axis=-1, keepdims=True) + RMS_EPS) * kvg_ref[...]
        pj["ckvb"] = ckv.astype(BF16)
        pj["k_rope"] = (p1[:, _P1_KR] * pj["cos"] + p1[:, _P1_KROT] * pj["sin"]).astype(BF16)

    def proj_up():
        qscale = (MLA_NOPE + MLA_ROPE) ** -0.5 * math.log2(math.e)
        cqb, ckvb = pj["cqb"], pj["ckvb"]
        pj["qt_nope"] = _nt_dot(wuq_ref[_UQ_NOPE, :], cqb) * qscale
        pj["qt_r"] = _nt_dot(wuq_ref[_UQ_ROPE, :], cqb)
        pj["qt_rot"] = _nt_dot(wuq_ref[_UQ_ROT, :], cqb)
        pj["k_nope"] = _dot(ckvb, wuk_ref[...]).astype(BF16)
        vt_ref[0] = _nt_dot(wuv_ref[...], ckvb).astype(BF16)
        pj["qscale"] = qscale

    def proj_mlstm_inputs():
        xb = xb_ref[...]
        pj["mv"] = _dot(xb, w_ref[:, _C_MV:_C_MO]).astype(BF16)
        pj["mo"] = _dot(xb, w_ref[:, _C_MO:PROJ_W]).astype(BF16)
        pj["gt"] = _nt_dot(wg_ref[...], xb) + gb_ref[...]

    def proj_write():
        cos_t = jnp.transpose(pj["cos"])
        sin_t = jnp.transpose(pj["sin"])
        for h in range(MLA_HEADS):
            rows = slice(h * LANES, (h + 1) * LANES)
            q_ref[h * HEAD_PAD:h * HEAD_PAD + LANES, :] = pj["qt_nope"][rows, :].astype(BF16)
            q_ref[h * HEAD_PAD + LANES:(h + 1) * HEAD_PAD, :] = (
                (pj["qt_r"][rows, :] * cos_t + pj["qt_rot"][rows, :] * sin_t) * pj["qscale"]).astype(BF16)
            k_ref[:, h * HEAD_PAD:h * HEAD_PAD + LANES] = pj["k_nope"][:, h * LANES:(h + 1) * LANES]
            k_ref[:, h * HEAD_PAD + LANES:(h + 1) * HEAD_PAD] = pj["k_rope"]
        mq_s[...] = pj["mq"]
        mk_s[...] = pj["mk"]
        mv_s[...] = pj["mv"]
        mo_s[...] = pj["mo"]
        gt_s[...] = pj["gt"]

    def scan_scores():
        ml["qs"] = [mq_s[:, h * ML_QK:(h + 1) * ML_QK] for h in heads]
        ml["ks"] = [mk_s[:, h * ML_QK:(h + 1) * ML_QK] for h in heads]
        ml["vs"] = [mv_s[:, h * ML_V:(h + 1) * ML_V] for h in heads]
        ml["cts"] = [c_sc[h] for h in heads]
        ml["qk"] = [_nt_dot(ml["qs"][h], ml["ks"][h]) for h in heads]
        ml["qc"] = [_dot(ml["qs"][h], ml["cts"][h].astype(BF16)) for h in heads]

    def scan_heads():
        g8 = gt_s[...]
        capped = GATE_CAP * jnp.tanh(g8 / GATE_CAP)
        logf = jnp.minimum(capped, 0.0) - jnp.log1p(jnp.exp(-jnp.abs(capped)))
        row8 = lax.broadcasted_iota(jnp.int32, g8.shape, 0)
        bsum = _lane_cumsum(jnp.where(row8 >= ML_HEADS, logf, 0.0))
        gb8 = jnp.where(row8 < ML_HEADS, capped - pltpu.roll(bsum, ML_HEADS, 0), bsum)
        rr = lax.broadcasted_iota(jnp.int32, (L, L), 0)
        cc = lax.broadcasted_iota(jnp.int32, (L, L), 1)
        causal = cc <= rr
        eye = cc == rr
        pending = []
        for h in heads:
            g_row = gb8[h:h + 1, :]
            b_row = gb8[ML_HEADS + h:ML_HEADS + h + 1, :]
            G = jnp.broadcast_to(g_row, (L, L))
            Bm = jnp.broadcast_to(b_row, (L, L))
            m_prev = m_sc[h:h + 1, 0:1]
            cummax = jnp.max(jnp.where(causal, G, -jnp.inf), axis=1, keepdims=True)
            mcol = jnp.maximum(m_prev, cummax)
            g_col = jnp.sum(jnp.where(eye, G, 0.0), axis=1, keepdims=True)
            b_col = jnp.sum(jnp.where(eye, Bm, 0.0), axis=1, keepdims=True)
            dmat = jnp.where(causal, jnp.exp(G - mcol), 0.0)
            inter_w = jnp.exp(m_prev - mcol)

            qh, kh, vh, ct = ml["qs"][h], ml["ks"][h], ml["vs"][h], ml["cts"][h]
            nrow = n_sc[h:h + 1, :]

            s = ml["qk"][h] * dmat
            num = _dot(s.astype(BF16), vh) + inter_w * ml["qc"][h]
            den = (jnp.sum(s, axis=1, keepdims=True)
                   + inter_w * jnp.sum(qh.astype(F32) * nrow, axis=1, keepdims=True))
            hval = num / jnp.maximum(jnp.abs(den), jnp.exp(-(b_col + mcol)))

            m_last = mcol[L - 1:L, :]
            decay = inter_w[L - 1:L, :]
            w_col = jnp.exp(g_col - m_last)
            kw = kh.astype(F32) * w_col
            pending.append((kw.astype(BF16), vh, decay, ct))
            n_sc[h:h + 1, :] = decay * nrow + jnp.sum(kw, axis=0, keepdims=True)
            m_sc[h:h + 1, :] = jnp.broadcast_to(b_row[:, L - 1:L] + m_last, (1, LANES))

            hn = hval * lax.rsqrt(jnp.mean(hval * hval, axis=-1, keepdims=True) + RMS_EPS)
            hn = hn * hg_ref[:, h * ML_V:(h + 1) * ML_V]
            og = jax.nn.sigmoid(mo_s[:, h * ML_V:(h + 1) * ML_V].astype(F32))
            mem_ref[:, h * ML_V:(h + 1) * ML_V] = (
                og * hn * beta_ref[:, h * ML_V:(h + 1) * ML_V]).astype(BF16)
        ml["pending"] = pending

    def scan_update():
        for h, (kwb, vh, decay, ct) in enumerate(ml["pending"]):
            upd = lax.dot_general(kwb, vh, (((0,), (0,)), ((), ())),
                                  preferred_element_type=F32)
            c_sc[h] = decay * ct + upd

    @pl.when(jnp.logical_and(g > 0, (g - 1) % chunks_per_seq == 0))
    def _():
        c_sc[...] = jnp.zeros(c_sc.shape, F32)
        n_sc[...] = jnp.zeros(n_sc.shape, F32)
        m_sc[...] = jnp.full(m_sc.shape, -jnp.inf, F32)

    @pl.when(g == 0)
    def _():
        proj_latents()
        proj_norms()
        proj_up()
        proj_mlstm_inputs()
        proj_write()

    @pl.when(jnp.logical_and(g > 0, g < n))
    def _():
        scan_scores()
        proj_latents()
        proj_mlstm_inputs()
        proj_norms()
        scan_heads()
        proj_up()
        scan_update()
        proj_write()

    @pl.when(g == n)
    def _():
        scan_scores()
        scan_heads()
        scan_update()


def _proj_mlstm_call(x2, pos_col, invf, w_all, wg_t, gbias, qg, kvg, wuq_all, wuk, wuv,
                     head_g, beta_ml, seq):
    T = x2.shape[0]
    tm = PROJ_TM
    assert tm == ML_L
    n = T // tm
    cur = lambda g: jnp.minimum(g, n - 1)
    row = lambda w: pl.BlockSpec((tm, w), lambda g: (cur(g), 0))
    out_shapes = (
        jax.ShapeDtypeStruct((MLA_HEADS * HEAD_PAD, T), BF16),
        jax.ShapeDtypeStruct((T, MLA_HEADS * HEAD_PAD), BF16),
        jax.ShapeDtypeStruct((T // ATT_T, MLA_OUT, ATT_T), BF16),
        jax.ShapeDtypeStruct((T, ML_OUT), BF16),
    )
    kern = functools.partial(_proj_mlstm_kernel, seq // ML_L)
    return pl.pallas_call(
        kern,
        grid=(n + 1,),
        in_specs=[
            row(D_MODEL),
            pl.BlockSpec((tm, 1), lambda g: (cur(g), 0)),
            _const_spec((1, LANES)),
            _const_spec((D_MODEL, PROJ_W)),
            _const_spec((8, D_MODEL)),
            _const_spec((8, 1)),
            _const_spec((1, Q_LORA)),
            _const_spec((1, KV_LORA)),
            _const_spec((3 * _UQ_ROWS, Q_LORA)),
            _const_spec((KV_LORA, MLA_HEADS * MLA_NOPE)),
            _const_spec((MLA_OUT, KV_LORA)),
            _const_spec((1, ML_OUT)),
            _const_spec((1, ML_OUT)),
        ],
        out_specs=(
            pl.BlockSpec((MLA_HEADS * HEAD_PAD, tm), lambda g: (0, cur(g))),
            row(MLA_HEADS * HEAD_PAD),
            pl.BlockSpec((1, MLA_OUT, tm),
                         lambda g: (cur(g) // (ATT_T // tm), 0, cur(g) % (ATT_T // tm))),
            pl.BlockSpec((tm, ML_OUT), lambda g: (jnp.maximum(g - 1, 0), 0)),
        ),
        out_shape=out_shapes,
        scratch_shapes=[pltpu.VMEM((tm, D_MODEL), BF16),
                        pltpu.VMEM((tm, ML_HEADS * ML_QK), BF16),
                        pltpu.VMEM((tm, ML_HEADS * ML_QK), BF16),
                        pltpu.VMEM((tm, ML_OUT), BF16),
                        pltpu.VMEM((tm, ML_OUT), BF16),
                        pltpu.VMEM((8, tm), F32),
                        pltpu.VMEM((ML_HEADS, ML_QK, ML_V), F32),
                        pltpu.VMEM((8, ML_QK), F32),
                        pltpu.VMEM((8, LANES), F32)],
        compiler_params=pltpu.CompilerParams(
            dimension_semantics=("arbitrary",), vmem_limit_bytes=V7X_VMEM_LIMIT_BYTES),
        name="proj_mlstm",
    )(x2, pos_col, invf, w_all, wg_t, gbias, qg, kvg, wuq_all, wuk, wuv, head_g, beta_ml)


def _scores(qt, k):
    return _dot(k, qt)


def _chunk_causal(s):
    t = s.shape[0]
    kc = lax.broadcasted_iota(jnp.int32, (t, t), 0) // CHUNK
    qc = lax.broadcasted_iota(jnp.int32, (t, t), 1) // CHUNK
    return jnp.where(kc <= qc, s, -jnp.inf)


def _softmax_update(state, s, vt, smax=None):
    m_prev, l_prev, acc = state
    if smax is None:
        smax = jnp.max(s, axis=0, keepdims=True)
    m_new = jnp.maximum(m_prev, smax)
    alpha = jnp.exp2(m_prev - m_new)
    p = jnp.exp2(s - m_new)
    l_new = alpha * l_prev + jnp.sum(p, axis=0, keepdims=True)
    acc_new = alpha * acc + _dot(vt, p.astype(BF16))
    return m_new, l_new, acc_new


def _flash_kernel(q_ref, k_ref, vt_ref, beta_ref, qn_ref, kf_ref, o_ref, m_sc, l_sc, acc_sc, s_sc,
                  mx_sc):
    i = pl.program_id(2)
    t = ATT_T
    m_sc[...] = jnp.full(m_sc.shape, NEG_BIG, F32)
    l_sc[...] = jnp.zeros(l_sc.shape, F32)
    acc_sc[...] = jnp.zeros(acc_sc.shape, F32)

    def keys(j):
        return k_ref[pl.ds(pl.multiple_of(j * t, t), t), :]

    def load(half):
        return m_sc[half], l_sc[half], acc_sc[half]

    def store(half, state):
        m_sc[half], l_sc[half], acc_sc[half] = state

    def put(tile, scores):
        s_sc[tile] = scores
        mx_sc[tile] = jnp.max(scores, axis=0, keepdims=True)

    q0 = q_ref[:, 0:t]
    q1 = q_ref[:, t:2 * t]

    first_step = (pl.program_id(0) == 0) & (pl.program_id(1) == 0) & (i == 0)

    @pl.when(first_step)
    def _():
        kn0, kn1 = keys(0), keys(1)
        put(0, _scores(q0, kn0))
        put(1, _scores(q1, kn0))
        put(2, _scores(q0, kn1))
        put(3, _scores(q1, kn1))

    def body(jj, carry):
        v0, v1 = vt_ref[2 * jj], vt_ref[2 * jj + 1]
        kn0, kn1 = keys(2 * jj + 2), keys(2 * jj + 3)
        sn0 = _scores(q0, kn0)
        sn1 = _scores(q1, kn0)
        st0 = _softmax_update(load(0), s_sc[0], v0, mx_sc[0])
        put(0, sn0)
        st1 = _softmax_update(load(1), s_sc[1], v0, mx_sc[1])
        put(1, sn1)
        sn2 = _scores(q0, kn1)
        store(0, _softmax_update(st0, s_sc[2], v1, mx_sc[2]))
        put(2, sn2)
        store(1, _softmax_update(st1, s_sc[3], v1, mx_sc[3]))
        put(3, _scores(q1, kn1))
        return carry

    def body2(u, carry):
        body(2 * u, carry)
        return body(2 * u + 1, carry)

    lax.fori_loop(0, i // 2, body2, 0)

    @pl.when(i % 2 == 1)
    def _():
        body(i - 1, 0)

    v0, v1 = vt_ref[2 * i], vt_ref[2 * i + 1]
    qn0, qn1 = qn_ref[:, 0:t], qn_ref[:, t:2 * t]
    kf0, kf1 = kf_ref[0:t, :], kf_ref[t:2 * t, :]
    sn0 = _scores(qn0, kf0)
    sn1 = _scores(qn1, kf0)
    st0 = _softmax_update(load(0), _chunk_causal(s_sc[0]), v0)
    put(0, sn0)
    st1 = _softmax_update(load(1), s_sc[1], v0, mx_sc[1])
    put(1, sn1)
    st1 = _softmax_update(st1, _chunk_causal(s_sc[3]), v1)
    sn2 = _scores(qn0, kf1)
    sn3 = _scores(qn1, kf1)
    for half, (_, l_fin, acc_fin) in enumerate((st0, st1)):
        out = jnp.transpose(acc_fin / l_fin) * beta_ref[...]
        o_ref[half * t:(half + 1) * t, :] = out.astype(BF16)
    put(2, sn2)
    put(3, sn3)


def _flash_call(qt_pad, k_pad, vt, beta_mla, batch, seq):
    T = k_pad.shape[0]
    t = ATT_T
    tq = 2 * t
    nq = seq // tq
    nkv = seq // t
    n_steps = batch * MLA_HEADS * nq

    def successor(b, h, i):
        g = jnp.minimum((b * MLA_HEADS + h) * nq + i + 1, n_steps - 1)
        return g // (nq * MLA_HEADS), (g // nq) % MLA_HEADS, g % nq

    def next_q(b, h, i):
        b2, h2, i2 = successor(b, h, i)
        return h2, b2 * nq + i2

    def next_first_keys(b, h, i):
        b2, h2, _ = successor(b, h, i)
        return b2 * (seq // tq), h2

    return pl.pallas_call(
        _flash_kernel,
        grid=(batch, MLA_HEADS, nq),
        in_specs=[
            pl.BlockSpec((HEAD_PAD, tq), lambda b, h, i: (h, b * nq + i)),
            pl.BlockSpec((seq, HEAD_PAD), lambda b, h, i: (b, h)),
            pl.BlockSpec((nkv, MLA_V, t), lambda b, h, i: (b, h, 0)),
            pl.BlockSpec((1, MLA_V), lambda b, h, i: (0, h)),
            pl.BlockSpec((HEAD_PAD, tq), next_q),
            pl.BlockSpec((tq, HEAD_PAD), next_first_keys),
        ],
        out_specs=pl.BlockSpec((tq, MLA_V), lambda b, h, i: (b * nq + i, h)),
        out_shape=jax.ShapeDtypeStruct((T, MLA_OUT), BF16),
        scratch_shapes=[pltpu.VMEM((2, 1, t), F32), pltpu.VMEM((2, 1, t), F32),
                        pltpu.VMEM((2, MLA_V, t), F32), pltpu.VMEM((4, t, t), F32),
                        pltpu.VMEM((4, 1, t), F32)],
        compiler_params=pltpu.CompilerParams(
            dimension_semantics=("arbitrary", "arbitrary", "arbitrary"),
            vmem_limit_bytes=V7X_VMEM_LIMIT_BYTES),
        name="mla_flash",
    )(qt_pad, k_pad, vt, beta_mla, qt_pad, k_pad)


def _layernorm(y, g, b):
    mu = jnp.mean(y, axis=-1, keepdims=True)
    yc = y - mu
    return yc * lax.rsqrt(jnp.mean(yc * yc, axis=-1, keepdims=True) + LN_EPS) * g + b


def _outproj_kernel(a_ref, m_ref, x_ref, w_ref, g_ref, b_ref, wg32_ref, wv32_ref, wd32_ref,
                    o_ref, wgb_ref, wvb_ref, wdb_ref):
    wgb_ref[...] = wg32_ref[...].astype(BF16)
    wvb_ref[...] = wv32_ref[...].astype(BF16)
    wdb_ref[...] = wd32_ref[...].astype(BF16)

    tm = o_ref.shape[0]
    sub = tm // OUT_SUBTILES

    def rows(r):
        return slice(r * sub, (r + 1) * sub)

    def project(r):
        return _dot(a_ref[rows(r), :], w_ref[0:MLA_OUT, :]) + _dot(m_ref[rows(r), :], w_ref[MLA_OUT:, :])

    def finish(r, acc):
        o_ref[rows(r), :] = _layernorm(DN_ALPHA * x_ref[rows(r), :] + acc, g_ref[...], b_ref[...])

    acc = project(0)
    for r in range(1, OUT_SUBTILES):
        nxt = project(r)
        finish(r - 1, acc)
        acc = nxt
    finish(OUT_SUBTILES - 1, acc)


def _outproj_call(attn, mem, x2, w_out, g, b, wg32, wv32, wd32):
    T = x2.shape[0]
    tm = OUT_TM
    n = T // tm
    assert D_MODEL % n == 0 and D_FF % n == 0
    up_rows, down_rows = D_MODEL // n, D_FF // n
    slab = lambda rows, cols: pl.BlockSpec((rows, cols), lambda i: (i, 0))
    return pl.pallas_call(
        _outproj_kernel,
        grid=(n,),
        in_specs=[
            pl.BlockSpec((tm, MLA_OUT), lambda i: (i, 0)),
            pl.BlockSpec((tm, ML_OUT), lambda i: (i, 0)),
            pl.BlockSpec((tm, D_MODEL), lambda i: (i, 0)),
            _const_spec((MLA_OUT + ML_OUT, D_MODEL)),
            _const_spec((1, D_MODEL)),
            _const_spec((1, D_MODEL)),
            slab(up_rows, D_FF), slab(up_rows, D_FF), slab(down_rows, D_MODEL),
        ],
        out_specs=(pl.BlockSpec((tm, D_MODEL), lambda i: (i, 0)),
                   slab(up_rows, D_FF), slab(up_rows, D_FF), slab(down_rows, D_MODEL)),
        out_shape=(jax.ShapeDtypeStruct((T, D_MODEL), F32),
                   jax.ShapeDtypeStruct((D_MODEL, D_FF), BF16),
                   jax.ShapeDtypeStruct((D_MODEL, D_FF), BF16),
                   jax.ShapeDtypeStruct((D_FF, D_MODEL), BF16)),
        compiler_params=pltpu.CompilerParams(
            dimension_semantics=("arbitrary",), vmem_limit_bytes=V7X_VMEM_LIMIT_BYTES),
        name="outproj_ln",
    )(attn, mem, x2, w_out, g, b, wg32, wv32, wd32)


def _gelu_tanh(x):
    c = math.sqrt(2.0 / math.pi)
    return 0.5 * x * (1.0 + jnp.tanh(c * (x + 0.044715 * (x * x * x))))


def _ffn_kernel(tiles_per_seq, h_ref, wg_ref, wv_ref, wd_ref, cw_ref, cb_ref, g_ref, b_ref,
                o_ref, hb_ref, gbuf_ref, val_ref, halo_ref):
    i = pl.program_id(0)
    f = pl.program_id(1)
    nf = pl.num_programs(1) - 1
    tm = FFN_TM
    cols = [slice(c * FFN_SUB, (c + 1) * FFN_SUB) for c in range(FFN_TF // FFN_SUB)]
    first = (i % tiles_per_seq) == 0

    def up_project(c):
        hb = hb_ref[...]
        return _dot(hb, wg_ref[:, c]), _dot(hb, wv_ref[:, c])

    def stash(c, gate, val):
        gbuf_ref[0:8, c] = jnp.where(first, 0.0, halo_ref[f, :, c])
        gbuf_ref[8:8 + tm, c] = gate
        halo_ref[f, :, c] = gate[tm - 8:tm, :]
        val_ref[:, c] = val

    def down_project(c, r0=0, nr=tm):
        conv = (cw_ref[0:1, c] * gbuf_ref[6 + r0:6 + r0 + nr, c]
                + cw_ref[1:2, c] * gbuf_ref[7 + r0:7 + r0 + nr, c]
                + cw_ref[2:3, c] * gbuf_ref[8 + r0:8 + r0 + nr, c] + cb_ref[:, c])
        act = (_gelu_tanh(conv) * val_ref[r0:r0 + nr, c]).astype(BF16)
        return _dot(act, wd_ref[c, :])

    def consume(r0=0, nr=tm):
        acc = None
        for c in cols:
            part = down_project(c, r0, nr)
            acc = part if acc is None else acc + part
        return o_ref[r0:r0 + nr, :] + acc

    @pl.when(jnp.logical_and(i == 0, f == 0))
    def _():
        halo_ref[...] = jnp.zeros(halo_ref.shape, F32)

    @pl.when(f == 0)
    def _():
        hb_ref[...] = h_ref[...].astype(BF16)
        o_ref[...] = DN_ALPHA * h_ref[...]
        for c in cols:
            stash(c, *up_project(c))

    @pl.when(jnp.logical_and(f > 0, f < nf))
    def _():
        ups = [up_project(c) for c in cols]
        o_ref[...] = consume()
        for c, (gate, val) in zip(cols, ups):
            stash(c, gate, val)

    @pl.when(f == nf)
    def _():
        nr = tm // FFN_LAST_SUBTILES
        for r in range(FFN_LAST_SUBTILES):
            rows = slice(r * nr, (r + 1) * nr)
            o_ref[rows, :] = _layernorm(consume(r * nr, nr), g_ref[...], b_ref[...])


def _ffn_call(h1, wg, wv, wd, conv_w, conv_b, g, b, seq):
    T = h1.shape[0]
    tm, tf = FFN_TM, FFN_TF
    nf = D_FF // tf
    kern = functools.partial(_ffn_kernel, seq // tm)
    produced = lambda i, f: (0, jnp.minimum(f, nf - 1))
    consumed = lambda i, f: (0, jnp.maximum(f - 1, 0))
    return pl.pallas_call(
        kern,
        grid=(T // tm, nf + 1),
        in_specs=[
            pl.BlockSpec((tm, D_MODEL), lambda i, f: (i, 0)),
            pl.BlockSpec((D_MODEL, tf), produced),
            pl.BlockSpec((D_MODEL, tf), produced),
            pl.BlockSpec((tf, D_MODEL), lambda i, f: (jnp.maximum(f - 1, 0), 0)),
            pl.BlockSpec((3, tf), consumed),
            pl.BlockSpec((1, tf), consumed),
            _const_spec((1, D_MODEL)),
            _const_spec((1, D_MODEL)),
        ],
        out_specs=pl.BlockSpec((tm, D_MODEL), lambda i, f: (i, 0)),
        out_shape=jax.ShapeDtypeStruct((T, D_MODEL), F32),
        scratch_shapes=[pltpu.VMEM((tm, D_MODEL), BF16),
                        pltpu.VMEM((tm + 8, tf), F32),
                        pltpu.VMEM((tm, tf), F32),
                        pltpu.VMEM((nf, 8, tf), F32)],
        compiler_params=pltpu.CompilerParams(
            dimension_semantics=("arbitrary", "arbitrary"),
            vmem_limit_bytes=V7X_VMEM_LIMIT_BYTES),
        name="convglu_ffn_ln",
    )(h1, wg, wv, wd, conv_w, conv_b, g, b)


def _prep_proj_weights(w_in, w_uq, b_igate, b_fgate):
    half = MLA_ROPE // 2
    o = 0
    w_cq = w_in[:, o:o + Q_LORA]; o += Q_LORA
    w_ckv = w_in[:, o:o + KV_LORA]; o += KV_LORA
    w_kr = w_in[:, o:o + MLA_ROPE]; o += MLA_ROPE
    w_mq = w_in[:, o:o + ML_HEADS * ML_QK]; o += ML_HEADS * ML_QK
    w_mk = w_in[:, o:o + ML_HEADS * ML_QK]; o += ML_HEADS * ML_QK
    w_mv = w_in[:, o:o + ML_OUT]; o += ML_OUT
    w_mo = w_in[:, o:o + ML_OUT]; o += ML_OUT
    w_gates = w_in[:, o:o + 2 * ML_HEADS]
    w_krot = jnp.concatenate([-w_kr[:, half:], w_kr[:, :half]], axis=1)
    z64 = jnp.zeros((D_MODEL, LANES - MLA_ROPE), w_in.dtype)
    w_all = jnp.concatenate([w_cq, w_ckv, w_kr, z64, w_krot, z64, w_mq, w_mk, w_mv, w_mo], axis=1)
    wg_t = w_gates.T
    gbias = jnp.concatenate([b_igate, b_fgate])[:, None]

    uq = w_uq.reshape(Q_LORA, MLA_HEADS, MLA_NOPE + MLA_ROPE)
    uq_nope = uq[:, :, :MLA_NOPE].reshape(Q_LORA, MLA_HEADS * MLA_NOPE)
    uq_r = uq[:, :, MLA_NOPE:]
    uq_rot = jnp.concatenate([-uq_r[:, :, half:], uq_r[:, :, :half]], axis=2)
    zpad = jnp.zeros((Q_LORA, MLA_HEADS, LANES - MLA_ROPE), w_uq.dtype)
    uq_r = jnp.concatenate([uq_r, zpad], axis=2).reshape(Q_LORA, MLA_HEADS * LANES)
    uq_rot = jnp.concatenate([uq_rot, zpad], axis=2).reshape(Q_LORA, MLA_HEADS * LANES)
    wuq_all = jnp.concatenate([uq_nope, uq_r, uq_rot], axis=1).T
    return w_all.astype(BF16), wg_t.astype(BF16), gbias.astype(F32), wuq_all.astype(BF16)


def kernel(x, positions, w_in, q_norm_g, kv_norm_g, w_uq, w_uk, w_uv, b_igate, b_fgate,
           ml_head_g, beta_mla, beta_ml, w_out, ln1_g, ln1_b, w_ffn_gate, w_ffn_val,
           conv_w, conv_b, w_down, ln2_g, ln2_b):
    B, S, D = x.shape
    T = B * S
    assert D == D_MODEL and w_in.shape[0] == DEPTH == 1
    assert S % (2 * ATT_T) == 0 and S % ML_L == 0 and S % FFN_TM == 0 and T % PROJ_TM == 0
    l = 0
    x2 = x.reshape(T, D)
    pos_col = positions.reshape(T, 1).astype(F32)
    inv_freq = 1.0 / (ROPE_BASE ** (jnp.arange(0, MLA_ROPE, 2, dtype=F32) / MLA_ROPE))
    invf = jnp.tile(inv_freq, LANES // (MLA_ROPE // 2))[None, :]

    w_all, wg_t, gbias, wuq_all = _prep_proj_weights(w_in[l], w_uq[l], b_igate[l], b_fgate[l])
    qt_pad, k_pad, vt, mem = _proj_mlstm_call(
        x2, pos_col, invf, w_all, wg_t, gbias,
        q_norm_g[l][None, :], kv_norm_g[l][None, :], wuq_all,
        w_uk[l].astype(BF16), w_uv[l].T.astype(BF16),
        ml_head_g[l][None, :], beta_ml[l][None, :], S)

    attn = _flash_call(qt_pad, k_pad, vt, beta_mla[l][None, :], B, S)
    h1, wg_b, wv_b, wd_b = _outproj_call(
        attn, mem, x2, w_out[l].astype(BF16), ln1_g[l][None, :], ln1_b[l][None, :],
        w_ffn_gate[l], w_ffn_val[l], w_down[l])
    out = _ffn_call(h1, wg_b, wv_b, wd_b, conv_w[l], conv_b[l][None, :],
                    ln2_g[l][None, :], ln2_b[l][None, :], S)
    return out.reshape(B, S, D)
```
